```python
import math
import jax, jax.numpy as jnp
from jax import lax
import numpy as np

D_MODEL = 1024
BATCH = 8
SEQ = 2048
DEPTH = 2
DEC_BATCH = 32
DEC_SEQ = 8
PAST_LEN = 16384
PAGE_SIZE = 128

HEAD_DIM = 64
N_SB_HEADS = 8
N_DF_HEADS = 4
SB_WIDTH = N_SB_HEADS * HEAD_DIM
DF_WIDTH = N_DF_HEADS * 2 * HEAD_DIM
D_FF = 2816
ROPE_THETA = 10000.0
Q_BLOCK = 128
NORM_EPS = 1e-6
SUBLN_EPS = 1e-5
N_SUBLAYERS = 3
PROJ_WIDTHS = (SB_WIDTH, SB_WIDTH, SB_WIDTH, DF_WIDTH, DF_WIDTH, DF_WIDTH, D_MODEL, D_MODEL)
IN_WIDTH = sum(PROJ_WIDTHS)
SPLIT_POINTS = tuple(sum(PROJ_WIDTHS[:i + 1]) for i in range(len(PROJ_WIDTHS) - 1))

kernel_name = 'hybrid_stickbreak_diffattn_macaron_decode_step'


def rms_norm(x, w, eps=NORM_EPS):
    xf = x.astype(jnp.float32)
    y = xf * lax.rsqrt(jnp.mean(jnp.square(xf), axis=-1, keepdims=True) + eps)
    return (y * w.astype(jnp.float32)).astype(x.dtype)


def ada_norm(x, w, shift, scale):
    return rms_norm(x, w) * (1 + scale[:, None, :]) + shift[:, None, :]


def swiglu(h, w_in, w_out):
    g, u = jnp.split(h @ w_in, 2, axis=-1)
    return (jax.nn.silu(g) * u) @ w_out


def rope(x, pos):
    half = HEAD_DIM // 2
    inv = 1.0 / (ROPE_THETA ** (jnp.arange(half, dtype=jnp.float32) / half))
    ang = pos.astype(jnp.float32)[:, None] * inv[None, :]
    ang = ang.reshape((ang.shape[0],) + (1,) * (x.ndim - 3) + (half,))
    cos, sin = jnp.cos(ang), jnp.sin(ang)
    xf = x.astype(jnp.float32)
    x1, x2 = xf[..., :half], xf[..., half:]
    return jnp.concatenate([x1 * cos - x2 * sin, x2 * cos + x1 * sin], axis=-1).astype(x.dtype)


def stick_breaking(q, k, v, q_pos, k_pos):
    allowed = k_pos[None, :] < q_pos[:, None]
    z = jnp.einsum('bqhd,bkhd->bhqk', q, k).astype(jnp.float32) * (HEAD_DIM ** -0.5)
    log_keep = jnp.where(allowed, jax.nn.log_sigmoid(-z), 0.0)
    log_between = lax.cumsum(log_keep, axis=3, reverse=True) - log_keep
    w = jnp.where(allowed, jnp.exp(jax.nn.log_sigmoid(z) + log_between), 0.0)
    return jnp.einsum('bhqk,bkhd->bqhd', w.astype(v.dtype), v)


def diff_attention(q, k, v, q_pos, k_pos, lam):
    causal = k_pos[None, :] <= q_pos[:, None]
    s = jnp.einsum('bqhnd,bkhnd->bhnqk', q, k).astype(jnp.float32) * (HEAD_DIM ** -0.5)
    p = jax.nn.softmax(jnp.where(causal, s, -jnp.inf), axis=-1)
    a = p[:, :, 0] - lam * p[:, :, 1]
    return jnp.einsum('bhqk,bkhe->bqhe', a.astype(v.dtype), v)


def prompt_mixers(q_sb, k_sb, v_sb, q_df, k_df, v_df, lam):
    seq = q_sb.shape[1]
    k_pos = jnp.arange(seq, dtype=jnp.int32)
    n_blk = seq // Q_BLOCK

    def block(i):
        start = i * Q_BLOCK
        q_pos = start + jnp.arange(Q_BLOCK, dtype=jnp.int32)
        qs = lax.dynamic_slice_in_dim(q_sb, start, Q_BLOCK, axis=1)
        qd = lax.dynamic_slice_in_dim(q_df, start, Q_BLOCK, axis=1)
        return (stick_breaking(qs, k_sb, v_sb, q_pos, k_pos),
                diff_attention(qd, k_df, v_df, q_pos, k_pos, lam))

    o_sb, o_df = lax.map(block, jnp.arange(n_blk, dtype=jnp.int32))
    o_sb = jnp.moveaxis(o_sb, 0, 1).reshape(q_sb.shape)
    o_df = jnp.moveaxis(o_df, 0, 1).reshape(q_df.shape[:3] + (2 * HEAD_DIM,))
    return o_sb, o_df


def sample_mixers(layer, q_sb, k_sb, v_sb, q_df, k_df, v_df, lam,
                  cache_sba_k, cache_sba_v, cache_diff_k, cache_diff_v, page_table):
    n_new = q_sb.shape[1]
    past = page_table.shape[1] * PAGE_SIZE
    k_pos = jnp.arange(past + n_new, dtype=jnp.int32)
    q_pos = past + jnp.arange(n_new, dtype=jnp.int32)

    def one_seq(args):
        pt, qs, ks, vs, qd, kd, vd = args

        def gather(cache):
            rows = cache[layer, pt]
            return rows.reshape((past,) + rows.shape[2:])

        ks_all = jnp.concatenate([gather(cache_sba_k), ks], axis=0)[None]
        vs_all = jnp.concatenate([gather(cache_sba_v), vs], axis=0)[None]
        kd_past = gather(cache_diff_k).reshape(past, N_DF_HEADS, 2, HEAD_DIM)
        kd_all = jnp.concatenate([kd_past, kd], axis=0)[None]
        vd_all = jnp.concatenate([gather(cache_diff_v), vd], axis=0)[None]
        o_sb = stick_breaking(qs[None], ks_all, vs_all, q_pos, k_pos)[0]
        o_df = diff_attention(qd[None], kd_all, vd_all, q_pos, k_pos, lam)[0]
        return o_sb, o_df

    return lax.map(one_seq, (page_table, q_sb, k_sb, v_sb, q_df, k_df, v_df))


def run_group(x, c, pos, attend, norm_w, w_ada, b_ada, ffn_w_in, ffn_w_out, w_in,
              w_o_sba, w_o_diff, w_out, diff_lambda, diff_subln_w, final_norm_w):
    bsz, seq, _ = x.shape
    sba_k, sba_v, diff_k, diff_v = [], [], [], []
    for l in range(DEPTH):
        mod = (jax.nn.silu(c) @ w_ada[l] + b_ada[l]).reshape(bsz, N_SUBLAYERS, 3, D_MODEL)
        shift, scale, gate = mod[:, :, 0], mod[:, :, 1], mod[:, :, 2]

        h = ada_norm(x, norm_w[l, 0], shift[:, 0], scale[:, 0])
        x = x + 0.5 * gate[:, 0][:, None] * swiglu(h, ffn_w_in[l, 0], ffn_w_out[l, 0])

        h = ada_norm(x, norm_w[l, 1], shift[:, 1], scale[:, 1])
        q_sb, k_sb, v_sb, q_df, k_df, v_df, g_sb, g_df = jnp.split(h @ w_in[l], SPLIT_POINTS, axis=-1)
        q_sb = q_sb.reshape(bsz, seq, N_SB_HEADS, HEAD_DIM)
        k_sb = k_sb.reshape(bsz, seq, N_SB_HEADS, HEAD_DIM)
        v_sb = v_sb.reshape(bsz, seq, N_SB_HEADS, HEAD_DIM)
        q_df = rope(q_df.reshape(bsz, seq, N_DF_HEADS, 2, HEAD_DIM), pos)
        k_df = rope(k_df.reshape(bsz, seq, N_DF_HEADS, 2, HEAD_DIM), pos)
        v_df = v_df.reshape(bsz, seq, N_DF_HEADS, 2 * HEAD_DIM)

        lam_init = 0.8 - 0.6 * math.exp(-0.3 * l)
        lp = diff_lambda[l].astype(jnp.float32)
        lam = jnp.exp(jnp.sum(lp[0] * lp[1])) - jnp.exp(jnp.sum(lp[2] * lp[3])) + lam_init

        o_sb, o_df = attend(l, q_sb, k_sb, v_sb, q_df, k_df, v_df, lam)
        o_df = rms_norm(o_df, diff_subln_w[l], SUBLN_EPS) * (1.0 - lam_init)
        o_sb = o_sb.reshape(bsz, seq, SB_WIDTH)
        o_df = o_df.reshape(bsz, seq, DF_WIDTH)
        merged = (jax.nn.sigmoid(g_sb) * (o_sb @ w_o_sba[l])
                  + jax.nn.sigmoid(g_df) * (o_df @ w_o_diff[l]))
        x = x + gate[:, 1][:, None] * (merged @ w_out[l])

        h = ada_norm(x, norm_w[l, 2], shift[:, 2], scale[:, 2])
        x = x + 0.5 * gate[:, 2][:, None] * swiglu(h, ffn_w_in[l, 1], ffn_w_out[l, 1])

        sba_k.append(k_sb)
        sba_v.append(v_sb)
        diff_k.append(k_df.reshape(bsz, seq, N_DF_HEADS, 2 * HEAD_DIM))
        diff_v.append(v_df)
    y = rms_norm(x, final_norm_w)
    return y, jnp.stack(sba_k), jnp.stack(sba_v), jnp.stack(diff_k), jnp.stack(diff_v)


def setup_inputs(seed: int = 0) -> dict:
    key = jax.random.key(seed)
    ks = jax.random.split(key, 24)
    f32 = jnp.float32
    n_pages = PAST_LEN // PAGE_SIZE
    n_used = DEC_BATCH * n_pages
    n_pool = n_used + n_used // 4

    def nrm(k, shape, s):
        return jax.random.normal(k, shape, f32) * s

    page_table = jax.random.permutation(ks[8], n_pool)[:n_used].reshape(DEC_BATCH, n_pages).astype(jnp.int32)
    return {
        'x_prompt': nrm(ks[0], (BATCH, SEQ, D_MODEL), 1.0),
        'x_sample': nrm(ks[1], (DEC_BATCH, DEC_SEQ, D_MODEL), 1.0),
        'c_prompt': nrm(ks[2], (BATCH, D_MODEL), 1.0),
        'c_sample': nrm(ks[3], (DEC_BATCH, D_MODEL), 1.0),
        'cache_sba_k': nrm(ks[4], (DEPTH, n_pool, PAGE_SIZE, N_SB_HEADS, HEAD_DIM), 1.0),
        'cache_sba_v': nrm(ks[5], (DEPTH, n_pool, PAGE_SIZE, N_SB_HEADS, HEAD_DIM), 1.0),
        'cache_diff_k': nrm(ks[6], (DEPTH, n_pool, PAGE_SIZE, N_DF_HEADS, 2 * HEAD_DIM), 1.0),
        'cache_diff_v': nrm(ks[7], (DEPTH, n_pool, PAGE_SIZE, N_DF_HEADS, 2 * HEAD_DIM), 1.0),
        'page_table': page_table,
        'norm_w': 1.0 + nrm(ks[9], (DEPTH, N_SUBLAYERS, D_MODEL), 0.02),
        'w_ada': nrm(ks[10], (DEPTH, D_MODEL, N_SUBLAYERS * 3 * D_MODEL), 0.5 * D_MODEL ** -0.5),
        'b_ada': nrm(ks[11], (DEPTH, N_SUBLAYERS * 3 * D_MODEL), 0.02),
        'ffn_w_in': nrm(ks[12], (DEPTH, 2, D_MODEL, 2 * D_FF), D_MODEL ** -0.5),
        'ffn_w_out': nrm(ks[13], (DEPTH, 2, D_FF, D_MODEL), D_FF ** -0.5),
        'w_in': nrm(ks[14], (DEPTH, D_MODEL, IN_WIDTH), D_MODEL ** -0.5),
        'w_o_sba': nrm(ks[15], (DEPTH, SB_WIDTH, D_MODEL), SB_WIDTH ** -0.5),
        'w_o_diff': nrm(ks[16], (DEPTH, DF_WIDTH, D_MODEL), DF_WIDTH ** -0.5),
        'w_out': nrm(ks[17], (DEPTH, D_MODEL, D_MODEL), D_MODEL ** -0.5),
        'diff_lambda': nrm(ks[18], (DEPTH, 4, HEAD_DIM), 0.1),
        'diff_subln_w': 1.0 + nrm(ks[19], (DEPTH, 2 * HEAD_DIM), 0.02),
        'final_norm_w': 1.0 + nrm(ks[20], (D_MODEL,), 0.02),
    }


def reference(x_prompt, x_sample, c_prompt, c_sample, cache_sba_k, cache_sba_v, cache_diff_k,
              cache_diff_v, page_table, norm_w, w_ada, b_ada, ffn_w_in, ffn_w_out, w_in,
              w_o_sba, w_o_diff, w_out, diff_lambda, diff_subln_w, final_norm_w):
    weights = (norm_w, w_ada, b_ada, ffn_w_in, ffn_w_out, w_in, w_o_sba, w_o_diff, w_out,
               diff_lambda, diff_subln_w, final_norm_w)

    def prompt_attend(l, q_sb, k_sb, v_sb, q_df, k_df, v_df, lam):
        return prompt_mixers(q_sb, k_sb, v_sb, q_df, k_df, v_df, lam)

    def sample_attend(l, q_sb, k_sb, v_sb, q_df, k_df, v_df, lam):
        return sample_mixers(l, q_sb, k_sb, v_sb, q_df, k_df, v_df, lam,
                             cache_sba_k, cache_sba_v, cache_diff_k, cache_diff_v, page_table)

    pos_prompt = jnp.arange(x_prompt.shape[1], dtype=jnp.int32)
    past = page_table.shape[1] * PAGE_SIZE
    pos_sample = past + jnp.arange(x_sample.shape[1], dtype=jnp.int32)

    y_prompt, sba_k_p, sba_v_p, diff_k_p, diff_v_p = run_group(
        x_prompt, c_prompt, pos_prompt, prompt_attend, *weights)
    y_sample, sba_k_s, sba_v_s, diff_k_s, diff_v_s = run_group(
        x_sample, c_sample, pos_sample, sample_attend, *weights)
    return (y_prompt, y_sample, sba_k_p, sba_v_p, diff_k_p, diff_v_p,
            sba_k_s, sba_v_s, diff_k_s, diff_v_s)
```

```python
import functools
import math

import jax
import jax.numpy as jnp
from jax import lax
from jax.experimental import pallas as pl
from jax.experimental.pallas import tpu as pltpu

HEAD_DIM = 64
N_SB_HEADS = 8
N_DF_HEADS = 4
PAGE_SIZE = 128
ROPE_THETA = 10000.0
NORM_EPS = 1e-6
SUBLN_EPS = 1e-5
N_SUBLAYERS = 3
SB_LOG_WEIGHT_FLOOR = -104.0
V7X_VMEM_LIMIT_BYTES = 56 * 1024 * 1024
MASK_VALUE = -1e30

F32 = jnp.float32
BF16 = jnp.bfloat16


def _cparams(n_axes):
    return pltpu.CompilerParams(
        dimension_semantics=("arbitrary",) * n_axes,
        vmem_limit_bytes=V7X_VMEM_LIMIT_BYTES,
    )


def _resident(shape):
    nd = len(shape)
    return pl.BlockSpec(shape, lambda *_: (0,) * nd, pipeline_mode=pl.Buffered(1))


def _dot(a, b):
    return jnp.dot(a, b, preferred_element_type=F32)


def _dot_nt(a, b):
    return lax.dot_general(a, b, (((1,), (1,)), ((), ())), preferred_element_type=F32)


def _split_bf16(x):
    hi = x.astype(BF16)
    lo = (x - hi.astype(F32)).astype(BF16)
    return hi, lo


def _rms(x, eps):
    return x * lax.rsqrt(jnp.mean(x * x, axis=-1, keepdims=True) + eps)


def _sigmoid(x):
    return 1.0 / (1.0 + jnp.exp(-x))


def _ada_mod_kernel(c_ref, w_ref, b_ref, o_ref):
    c = c_ref[...]
    a_hi, a_lo = _split_bf16(c * _sigmoid(c))
    w_hi, w_lo = _split_bf16(w_ref[...])
    o_ref[...] = _dot(a_hi, w_hi) + (_dot(a_hi, w_lo) + _dot(a_lo, w_hi)) + b_ref[...]


def _ada_mod(c_all, w_ada, b_ada):
    n, d = c_all.shape
    depth, _, width = w_ada.shape
    tn = 1536
    assert width % tn == 0
    return pl.pallas_call(
        _ada_mod_kernel,
        grid=(depth, width // tn),
        in_specs=[
            pl.BlockSpec((n, d), lambda l, j: (0, 0)),
            pl.BlockSpec((None, d, tn), lambda l, j: (l, 0, j)),
            pl.BlockSpec((None, 1, tn), lambda l, j: (l, 0, j)),
        ],
        out_specs=pl.BlockSpec((None, n, tn), lambda l, j: (l, 0, j)),
        out_shape=jax.ShapeDtypeStruct((depth, n, width), F32),
        compiler_params=_cparams(2),
        name="ada_mod",
    )(c_all, w_ada, b_ada.reshape(depth, 1, width))


def _ffn_kernel(x_ref, nw_ref, shift_ref, scale_ref, gate_ref, wi_ref, wo_ref, fw_ref, o_ref, acc_ref, *, n_chunks, final_norm):
    x = x_ref[...]
    h = _rms(x, NORM_EPS) * nw_ref[...]
    hb = (h * (1.0 + scale_ref[...]) + shift_ref[...]).astype(BF16)
    acc_ref[...] = jnp.zeros_like(acc_ref)

    def chunk(j, carry):
        g = _dot(hb, wi_ref[0, j])
        u = _dot(hb, wi_ref[1, j])
        a = (g * _sigmoid(g) * u).astype(BF16)
        acc_ref[...] += _dot(a, wo_ref[j])
        return carry

    lax.fori_loop(0, n_chunks, chunk, 0)
    y = x + (0.5 * gate_ref[...]) * acc_ref[...]
    if final_norm:
        y = _rms(y, NORM_EPS) * fw_ref[...]
    o_ref[...] = y


def _ffn(x, norm_w, shift, scale, gate, wi, wo, final_w, *, tm, final_norm):
    t, d = x.shape
    groups, r, _ = shift.shape
    _, n_chunks, _, fc = wi.shape
    assert t % (tm * groups) == 0
    tiles_per_group = t // (tm * groups)
    mod_spec = pl.BlockSpec((None, r, d), lambda i: (i // tiles_per_group, 0, 0))
    row_spec = pl.BlockSpec((1, d), lambda i: (0, 0))
    return pl.pallas_call(
        functools.partial(_ffn_kernel, n_chunks=n_chunks, final_norm=final_norm),
        grid=(t // tm,),
        in_specs=[
            pl.BlockSpec((tm, d), lambda i: (i, 0)),
            row_spec, mod_spec, mod_spec, mod_spec,
            _resident(wi.shape), _resident(wo.shape),
            row_spec,
        ],
        out_specs=pl.BlockSpec((tm, d), lambda i: (i, 0)),
        out_shape=jax.ShapeDtypeStruct((t, d), F32),
        scratch_shapes=[pltpu.VMEM((tm, d), F32)],
        compiler_params=_cparams(1),
        name="ffn",
    )(x, norm_w.reshape(1, d), shift, scale, gate, wi, wo, final_w.reshape(1, d))


def _prep_ffn_weights(w_in, w_out, fc):
    d, two_f = w_in.shape
    f = two_f // 2
    assert f % fc == 0
    n_chunks = f // fc
    wi = w_in.reshape(d, 2, n_chunks, fc).transpose(1, 2, 0, 3).astype(BF16)
    wo = w_out.reshape(n_chunks, fc, d).astype(BF16)
    return wi, wo


def _rope(y, cos, sin_signed):
    width = y.shape[-1]
    half = HEAD_DIM // 2
    lane = lax.broadcasted_iota(jnp.int32, y.shape, 1)
    partner = jnp.where(lane % HEAD_DIM < half, pltpu.roll(y, width - half, 1), pltpu.roll(y, half, 1))
    return y * cos + partner * sin_signed


def _in_proj_kernel(*refs, transposed_kv):
    if transposed_kv:
        (x_ref, nw_ref, shift_ref, scale_ref, cos_ref, sin_ref, w_ref, wkv_ref,
         qsb_ref, ksb_ref, vsb_ref, qdf_ref, kdf_ref, vdf_ref, g_ref) = refs
    else:
        (x_ref, nw_ref, shift_ref, scale_ref, cos_ref, sin_ref, w_ref,
         qsb_ref, ksb_ref, vsb_ref, qdf_ref, kdf_ref, vdf_ref, g_ref) = refs
    sb = N_SB_HEADS * HEAD_DIM
    df = N_DF_HEADS * 2 * HEAD_DIM
    tm = x_ref.shape[0]
    h = _rms(x_ref[...], NORM_EPS) * nw_ref[...]
    hb = (h * (1.0 + scale_ref[...]) + shift_ref[...]).astype(BF16)
    cos = cos_ref[...]
    sin = sin_ref[...]
    qsb_ref[...] = _dot(hb, w_ref[:, 0:sb])
    q_df = _rope(_dot(hb, w_ref[:, 3 * sb:3 * sb + df]), cos, sin)
    k_df = _rope(_dot(hb, w_ref[:, 3 * sb + df:3 * sb + 2 * df]), cos, sin)
    v_df = _dot(hb, w_ref[:, 3 * sb + 2 * df:3 * sb + 3 * df])
    qdf_ref[...] = q_df
    g_ref[...] = _dot(hb, w_ref[:, 3 * sb + 3 * df:])
    if transposed_kv:
        ksb_ref[...] = _dot_nt(wkv_ref[0:sb, :], hb)
        vsb_ref[...] = _dot_nt(wkv_ref[sb:2 * sb, :], hb)
        width = 2 * HEAD_DIM
        for head in range(N_DF_HEADS):
            kdf_ref[pl.ds(head, tm, stride=N_DF_HEADS), :] = k_df[:, head * width:(head + 1) * width]
            vdf_ref[pl.ds(head, tm, stride=N_DF_HEADS), :] = v_df[:, head * width:(head + 1) * width]
    else:
        ksb_ref[...] = _dot(hb, w_ref[:, sb:2 * sb])
        vsb_ref[...] = _dot(hb, w_ref[:, 2 * sb:3 * sb])
        kdf_ref[...] = k_df
        vdf_ref[...] = v_df


def _in_proj_prompt(x, norm_w, shift, scale, cos, sin, w_b, wkv_t, kv_prev, layer, *, depth, batch, seq, tm):
    t, d = x.shape
    sb = N_SB_HEADS * HEAD_DIM
    df = N_DF_HEADS * 2 * HEAD_DIM
    tiles_per_seq = seq // tm
    tok = lambda i: (i, 0)
    const = lambda i: (0, 0)
    mod_spec = pl.BlockSpec((None, 1, d), lambda i: (i // tiles_per_seq, 0, 0))
    tab_spec = pl.BlockSpec((tm, df), lambda i: (i % tiles_per_seq, 0))
    in_specs = [
        pl.BlockSpec((tm, d), tok), pl.BlockSpec((1, d), const), mod_spec, mod_spec, tab_spec, tab_spec,
        _resident(w_b.shape), _resident(wkv_t.shape),
    ]
    args = [x, norm_w.reshape(1, d), shift, scale, cos, sin, w_b, wkv_t]
    aliases = {}
    if kv_prev is not None:
        for n, buf in enumerate(kv_prev):
            in_specs.append(pl.BlockSpec(memory_space=pl.ANY))
            aliases[len(args)] = (1, 2, 4, 5)[n]
            args.append(buf)
    kt_spec = pl.BlockSpec((None, None, sb, tm), lambda i: (layer, i // tiles_per_seq, 0, i % tiles_per_seq))
    il_spec = pl.BlockSpec((None, None, N_DF_HEADS * tm, 2 * HEAD_DIM), lambda i: (layer, i // tiles_per_seq, i % tiles_per_seq, 0))
    kt_shape = jax.ShapeDtypeStruct((depth, batch, sb, seq), F32)
    il_shape = jax.ShapeDtypeStruct((depth, batch, N_DF_HEADS * seq, 2 * HEAD_DIM), F32)

    def body(*refs):
        n_in = 8
        n_alias = len(aliases)
        _in_proj_kernel(*refs[:n_in], *refs[n_in + n_alias:], transposed_kv=True)

    return pl.pallas_call(
        body,
        grid=(t // tm,),
        in_specs=in_specs,
        out_specs=[
            pl.BlockSpec((tm, sb), tok), kt_spec, kt_spec,
            pl.BlockSpec((tm, df), tok), il_spec, il_spec,
            pl.BlockSpec((tm, 2 * d), tok),
        ],
        out_shape=[
            jax.ShapeDtypeStruct((t, sb), F32), kt_shape, kt_shape,
            jax.ShapeDtypeStruct((t, df), F32), il_shape, il_shape,
            jax.ShapeDtypeStruct((t, 2 * d), F32),
        ],
        input_output_aliases=aliases,
        compiler_params=_cparams(1),
        name="in_proj_prompt",
    )(*args)


def _in_proj_sample(x, norm_w, shift, scale, cos, sin, w_b):
    t, d = x.shape
    sb = N_SB_HEADS * HEAD_DIM
    df = N_DF_HEADS * 2 * HEAD_DIM
    full = lambda shape: pl.BlockSpec(shape, lambda i: (0,) * len(shape))
    widths = (sb, sb, sb, df, df, df, 2 * d)
    return pl.pallas_call(
        functools.partial(_in_proj_kernel, transposed_kv=False),
        grid=(1,),
        in_specs=[full((t, d)), full((1, d)), full((t, d)), full((t, d)), full((t, df)), full((t, df)), full(w_b.shape)],
        out_specs=[full((t, w)) for w in widths],
        out_shape=[jax.ShapeDtypeStruct((t, w), F32) for w in widths],
        compiler_params=_cparams(1),
        name="in_proj_sample",
    )(x, norm_w.reshape(1, d), shift, scale, cos, sin, w_b)


def _rope_tables(pos, n_heads):
    half = HEAD_DIM // 2
    inv = 1.0 / (ROPE_THETA ** (jnp.arange(half, dtype=F32) / half))
    ang = pos.astype(F32)[:, None] * inv[None, :]
    cos, sin = jnp.cos(ang), jnp.sin(ang)
    cos = jnp.tile(jnp.concatenate([cos, cos], axis=1), (1, n_heads))
    sin = jnp.tile(jnp.concatenate([-sin, sin], axis=1), (1, n_heads))
    return cos, sin


def _softplus(z):
    return jnp.maximum(z, 0.0) + jnp.log1p(jnp.exp(-jnp.abs(z)))


def _strict_lower_ones(n):
    row = lax.broadcasted_iota(jnp.int32, (n, n), 0)
    col = lax.broadcasted_iota(jnp.int32, (n, n), 1)
    return jnp.where(row > col, 1.0, 0.0).astype(BF16)


def _sb_block(z, allowed, carry, tri):
    log_keep_raw = -_softplus(z)
    log_keep = log_keep_raw if allowed is None else jnp.where(allowed, log_keep_raw, 0.0)
    hi, lo = _split_bf16(log_keep)
    log_between = _dot(hi, tri) + _dot(lo, tri)
    w = jnp.exp(z + log_keep_raw + log_between + carry)
    if allowed is not None:
        w = jnp.where(allowed, w, 0.0)
    return w, carry + jnp.sum(log_keep, axis=-1, keepdims=True)


def _softmax_block(s, v_b, state):
    m, l, acc = state
    m_new = jnp.maximum(m, jnp.max(s, axis=-1, keepdims=True))
    alpha = jnp.exp(m - m_new)
    p = jnp.exp(s - m_new)
    l = alpha * l + jnp.sum(p, axis=-1, keepdims=True)
    acc = alpha * acc + _dot(p.astype(BF16), v_b)
    return m_new, l, acc


def _diff_lambda(dl_ref, lam_init):
    lp = dl_ref[...]
    a = jnp.sum(lp[0:1] * lp[1:2], axis=-1, keepdims=True)
    b = jnp.sum(lp[2:3] * lp[3:4], axis=-1, keepdims=True)
    return jnp.exp(a) - jnp.exp(b) + lam_init


def _sub_norm(o, subw, lam_init):
    return _rms(o, SUBLN_EPS) * subw * (1.0 - lam_init)


def _lam_init(layer):
    return 0.8 - 0.6 * math.exp(-0.3 * layer)


def _prompt_attn_kernel(qsb_ref, kt_ref, vt_ref, qdf_ref, kdf_ref, vdf_ref, dl_ref, subw_ref, osb_ref, odf_ref, *, blk, lam_init):
    seq = qsb_ref.shape[0]
    n_blk = seq // blk
    p = pl.program_id(1)
    scale = HEAD_DIM ** -0.5
    lam = _diff_lambda(dl_ref, lam_init)
    subw = subw_ref[...]
    tri = _strict_lower_ones(blk)
    lane = lax.broadcasted_iota(jnp.int32, (blk, 2 * HEAD_DIM), 1)
    first_half = lane < HEAD_DIM
    row = lax.broadcasted_iota(jnp.int32, (2 * blk, blk), 0) % blk
    col = lax.broadcasted_iota(jnp.int32, (2 * blk, blk), 1)

    def split_rows(q):
        return jnp.concatenate([jnp.where(first_half, q, 0.0), jnp.where(first_half, 0.0, q)], axis=0).astype(BF16)

    def q_block(iq, carry):
        q0 = pl.multiple_of(iq * blk, blk)

        q2 = split_rows(qdf_ref[pl.ds(q0, blk), :] * scale)

        def diff_step(jk, state, masked):
            base = jk * (blk * N_DF_HEADS) + p
            k = kdf_ref[pl.ds(base, blk, stride=N_DF_HEADS), :].astype(BF16)
            v = vdf_ref[pl.ds(base, blk, stride=N_DF_HEADS), :].astype(BF16)
            s = _dot_nt(q2, k)
            if masked:
                s = jnp.where(col <= row, s, MASK_VALUE)
            return _softmax_block(s, v, state)

        state = (jnp.full((2 * blk, 1), MASK_VALUE, F32), jnp.zeros((2 * blk, 1), F32), jnp.zeros((2 * blk, 2 * HEAD_DIM), F32))
        state = lax.fori_loop(0, iq, lambda jk, st: diff_step(jk, st, False), state)
        _, l, acc = diff_step(iq, state, True)
        o = acc / l
        odf_ref[pl.ds(q0, blk), :] = _sub_norm(o[:blk] - lam * o[blk:], subw, lam_init)

        qs = split_rows(qsb_ref[pl.ds(q0, blk), :] * scale)

        def sb_step(jk, c, acc, masked):
            k0 = pl.multiple_of(jk * blk, blk)
            z = _dot(qs, kt_ref[:, pl.ds(k0, blk)].astype(BF16))
            w, c = _sb_block(z, (col < row) if masked else None, c, tri)
            return c, acc + _dot_nt(w.astype(BF16), vt_ref[:, pl.ds(k0, blk)].astype(BF16))

        c, acc = sb_step(iq, jnp.zeros((2 * blk, 1), F32), jnp.zeros((2 * blk, 2 * HEAD_DIM), F32), True)

        def sb_cond(st):
            jk, c, _ = st
            return jnp.logical_and(jk >= 0, jnp.max(c) >= SB_LOG_WEIGHT_FLOOR)

        def sb_body(st):
            jk, c, acc = st
            c, acc = sb_step(jk, c, acc, False)
            return jk - 1, c, acc

        _, _, acc = lax.while_loop(sb_cond, sb_body, (iq - 1, c, acc))
        osb_ref[pl.ds(q0, blk), :] = jnp.where(first_half, acc[:blk], acc[blk:])
        return carry

    lax.fori_loop(0, n_blk, q_block, 0)


def _prompt_attn(q_sb, kt, vt, q_df, kdf, vdf, diff_lambda, subln_w, layer, *, batch, seq, blk):
    t = q_sb.shape[0]
    width = 2 * HEAD_DIM
    tok_spec = pl.BlockSpec((seq, width), lambda b, p: (b, p))
    kt_spec = pl.BlockSpec((None, None, width, seq), lambda b, p: (layer, b, p, 0))
    il_spec = pl.BlockSpec((None, None, N_DF_HEADS * seq, width), lambda b, p: (layer, b, 0, 0))
    return pl.pallas_call(
        functools.partial(_prompt_attn_kernel, blk=blk, lam_init=_lam_init(layer)),
        grid=(batch, N_DF_HEADS),
        in_specs=[
            tok_spec, kt_spec, kt_spec, tok_spec, il_spec, il_spec,
            pl.BlockSpec((None, 4, HEAD_DIM), lambda b, p: (layer, 0, 0)),
            pl.BlockSpec((None, 1, width), lambda b, p: (layer, 0, 0)),
        ],
        out_specs=[tok_spec, tok_spec],
        out_shape=[jax.ShapeDtypeStruct((t, N_SB_HEADS * HEAD_DIM), F32), jax.ShapeDtypeStruct((t, N_DF_HEADS * width), F32)],
        compiler_params=_cparams(2),
        name="prompt_attn",
    )(q_sb, kt, vt, q_df, kdf, vdf, diff_lambda, subln_w.reshape(subln_w.shape[0], 1, width))


def _decode_sb_kernel(pt_ref, q_ref, kn_ref, vn_ref, kc_ref, vc_ref, o_ref, kbuf, vbuf, sem, *, layer, n_pages):
    b = pl.program_id(0)
    n_new, sb = q_ref.shape
    rows = N_SB_HEADS * n_new

    def page_copies(j, slot):
        page = pt_ref[b, j]
        return (pltpu.make_async_copy(kc_ref.at[layer, page], kbuf.at[slot], sem.at[0, slot]),
                pltpu.make_async_copy(vc_ref.at[layer, page], vbuf.at[slot], sem.at[1, slot]))

    def start_page(j, slot):
        for cp in page_copies(j, slot):
            cp.start()

    def wait_page(j, slot):
        for cp in page_copies(j, slot):
            cp.wait()

    start_page(n_pages - 1, (n_pages - 1) % 2)

    tri = _strict_lower_ones(PAGE_SIZE)
    row_head = lax.broadcasted_iota(jnp.int32, (rows, sb), 0) // n_new
    lane_head = lax.broadcasted_iota(jnp.int32, (rows, sb), 1) // HEAD_DIM
    own_head = row_head == lane_head
    q = jnp.tile(q_ref[...] * (HEAD_DIM ** -0.5), (N_SB_HEADS, 1))
    qbd = jnp.where(own_head, q, 0.0).astype(BF16)

    pad = jnp.zeros((PAGE_SIZE - n_new, sb), F32)
    k_new = jnp.concatenate([kn_ref[...], pad], axis=0).astype(BF16)
    v_new = jnp.concatenate([vn_ref[...], pad], axis=0).astype(BF16)
    query = lax.broadcasted_iota(jnp.int32, (rows, PAGE_SIZE), 0) % n_new
    col = lax.broadcasted_iota(jnp.int32, (rows, PAGE_SIZE), 1)
    w, c = _sb_block(_dot_nt(qbd, k_new), col < query, jnp.zeros((rows, 1), F32), tri)
    acc = _dot(w.astype(BF16), v_new)

    def cond(st):
        j, c, _ = st
        return jnp.logical_and(j >= 0, jnp.max(c) >= SB_LOG_WEIGHT_FLOOR)

    def body(st):
        j, c, acc = st
        slot = j % 2
        wait_page(j, slot)

        @pl.when(j > 0)
        def _():
            start_page(j - 1, 1 - slot)

        w, c = _sb_block(_dot(qbd, kbuf[slot].astype(BF16)), None, c, tri)
        return j - 1, c, acc + _dot_nt(w.astype(BF16), vbuf[slot].astype(BF16))

    j_end, _, acc = lax.while_loop(cond, body, (n_pages - 1, c, acc))

    @pl.when(j_end >= 0)
    def _():
        wait_page(j_end, j_end % 2)

    acc = jnp.where(own_head, acc, 0.0)
    out = acc[0:n_new]
    for head in range(1, N_SB_HEADS):
        out = out + acc[head * n_new:(head + 1) * n_new]
    o_ref[...] = out


def _decode_sb(page_table, q, k_new, v_new, cache_kt, cache_vt, layer, *, n_new):
    t, sb = q.shape
    n_seq, n_pages = page_table.shape
    tok = pl.BlockSpec((n_new, sb), lambda b, pt: (b, 0))
    any_spec = pl.BlockSpec(memory_space=pl.ANY)
    return pl.pallas_call(
        functools.partial(_decode_sb_kernel, layer=layer, n_pages=n_pages),
        grid_spec=pltpu.PrefetchScalarGridSpec(
            num_scalar_prefetch=1,
            grid=(n_seq,),
            in_specs=[tok, tok, tok, any_spec, any_spec],
            out_specs=tok,
            scratch_shapes=[
                pltpu.VMEM((2, sb, PAGE_SIZE), F32),
                pltpu.VMEM((2, sb, PAGE_SIZE), F32),
                pltpu.SemaphoreType.DMA((2, 2)),
            ],
        ),
        out_shape=jax.ShapeDtypeStruct((t, sb), F32),
        compiler_params=_cparams(1),
        name="decode_sb",
    )(page_table, q, k_new, v_new, cache_kt, cache_vt)


def _decode_diff_kernel(*refs, pages_per_step, lam_init):
    pt_ref, q_ref, kn_ref, vn_ref = refs[:4]
    k_refs = refs[4:4 + pages_per_step]
    v_refs = refs[4 + pages_per_step:4 + 2 * pages_per_step]
    dl_ref, subw_ref, o_ref, m_ref, l_ref, acc_ref, qall_ref, bias_ref = refs[4 + 2 * pages_per_step:]
    del pt_ref
    step = pl.program_id(1)
    n_new = q_ref.shape[0]
    width = 2 * HEAD_DIM
    rows = N_DF_HEADS * 2 * n_new
    page_rows = PAGE_SIZE * N_DF_HEADS

    @pl.when(step == 0)
    def _():
        m_ref[...] = jnp.full(m_ref.shape, MASK_VALUE, F32)
        l_ref[...] = jnp.zeros(l_ref.shape, F32)
        acc_ref[...] = jnp.zeros(acc_ref.shape, F32)
        q = q_ref[...] * (HEAD_DIM ** -0.5)
        first_half = lax.broadcasted_iota(jnp.int32, (n_new, width), 1) < HEAD_DIM
        pieces = []
        for head in range(N_DF_HEADS):
            qh = q[:, head * width:(head + 1) * width]
            pieces += [jnp.where(first_half, qh, 0.0), jnp.where(first_half, 0.0, qh)]
        qall_ref[...] = jnp.concatenate(pieces, axis=0).astype(BF16)
        row_head = lax.broadcasted_iota(jnp.int32, (rows, page_rows), 0) // (2 * n_new)
        col_head = lax.broadcasted_iota(jnp.int32, (rows, page_rows), 1) % N_DF_HEADS
        bias_ref[...] = jnp.where(row_head == col_head, 0.0, MASK_VALUE)

    qall = qall_ref[...]
    bias = bias_ref[...]
    s = jnp.concatenate([_dot_nt(qall, k_ref[...].astype(BF16)) + bias for k_ref in k_refs], axis=1)
    m = m_ref[...]
    m_new = jnp.maximum(m, jnp.max(s, axis=-1, keepdims=True))
    alpha = jnp.exp(m - m_new)
    p = jnp.exp(s - m_new).astype(BF16)
    l_ref[...] = alpha * l_ref[...] + jnp.sum(p.astype(F32), axis=-1, keepdims=True)
    pv = _dot(p[:, 0:page_rows], v_refs[0][...].astype(BF16))
    for r in range(1, pages_per_step):
        pv = pv + _dot(p[:, r * page_rows:(r + 1) * page_rows], v_refs[r][...].astype(BF16))
    acc_ref[...] = alpha * acc_ref[...] + pv
    m_ref[...] = m_new

    @pl.when(step == pl.num_programs(1) - 1)
    def _():
        pad = jnp.zeros((PAGE_SIZE - N_DF_HEADS * n_new, width), F32)
        k_new = jnp.concatenate([kn_ref[...], pad], axis=0).astype(BF16)
        v_new = jnp.concatenate([vn_ref[...], pad], axis=0).astype(BF16)
        r_idx = lax.broadcasted_iota(jnp.int32, (rows, PAGE_SIZE), 0)
        c_idx = lax.broadcasted_iota(jnp.int32, (rows, PAGE_SIZE), 1)
        allowed = jnp.logical_and(c_idx % N_DF_HEADS == r_idx // (2 * n_new), c_idx // N_DF_HEADS <= r_idx % n_new)
        s_new = jnp.where(allowed, _dot_nt(qall, k_new), MASK_VALUE)
        _, l, acc = _softmax_block(s_new, v_new, (m_ref[...], l_ref[...], acc_ref[...]))
        o = acc / l
        lam = _diff_lambda(dl_ref, lam_init)
        subw = subw_ref[...]
        for head in range(N_DF_HEADS):
            o1 = o[(2 * head) * n_new:(2 * head + 1) * n_new]
            o2 = o[(2 * head + 1) * n_new:(2 * head + 2) * n_new]
            o_ref[:, head * width:(head + 1) * width] = _sub_norm(o1 - lam * o2, subw, lam_init)


def _decode_diff(page_table, q, k_new, v_new, cache_k, cache_v, diff_lambda, subln_w, layer, *, n_new, pages_per_step):
    t, df = q.shape
    n_seq, n_pages = page_table.shape
    assert n_pages % pages_per_step == 0
    width = 2 * HEAD_DIM
    rows = N_DF_HEADS * 2 * n_new
    page_rows = PAGE_SIZE * N_DF_HEADS
    tok = pl.BlockSpec((n_new, df), lambda b, c, pt: (b, 0))
    new_spec = pl.BlockSpec((None, N_DF_HEADS * n_new, width), lambda b, c, pt: (b, 0, 0))

    def page_spec(r):
        return pl.BlockSpec((None, None, page_rows, width), lambda b, c, pt: (layer, pt[b, c * pages_per_step + r], 0, 0))

    page_specs = [page_spec(r) for r in range(pages_per_step)]
    return pl.pallas_call(
        functools.partial(_decode_diff_kernel, pages_per_step=pages_per_step, lam_init=_lam_init(layer)),
        grid_spec=pltpu.PrefetchScalarGridSpec(
            num_scalar_prefetch=1,
            grid=(n_seq, n_pages // pages_per_step),
            in_specs=[tok, new_spec, new_spec] + page_specs + page_specs + [
                pl.BlockSpec((None, 4, HEAD_DIM), lambda b, c, pt: (layer, 0, 0)),
                pl.BlockSpec((None, 1, width), lambda b, c, pt: (layer, 0, 0)),
            ],
            out_specs=tok,
            scratch_shapes=[
                pltpu.VMEM((rows, 1), F32), pltpu.VMEM((rows, 1), F32), pltpu.VMEM((rows, width), F32),
                pltpu.VMEM((rows, width), BF16), pltpu.VMEM((rows, page_rows), F32),
            ],
        ),
        out_shape=jax.ShapeDtypeStruct((t, df), F32),
        compiler_params=_cparams(2),
        name="decode_diff",
    )(page_table, q, k_new, v_new, *([cache_k] * pages_per_step), *([cache_v] * pages_per_step),
      diff_lambda, subln_w.reshape(subln_w.shape[0], 1, width))


def _merge_kernel(x_ref, osb_ref, odf_ref, g_ref, gate_ref, wos_ref, wod_ref, wout_ref, o_ref):
    d = x_ref.shape[1]
    g = g_ref[...]
    merged = (_sigmoid(g[:, :d]) * _dot(osb_ref[...].astype(BF16), wos_ref[...])
              + _sigmoid(g[:, d:]) * _dot(odf_ref[...].astype(BF16), wod_ref[...]))
    o_ref[...] = x_ref[...] + gate_ref[...] * _dot(merged.astype(BF16), wout_ref[...])


def _merge(x, o_sb, o_df, g, gate, wos, wod, wout, *, tm):
    t, d = x.shape
    groups, r, _ = gate.shape
    tiles_per_group = t // (tm * groups)
    tok = lambda w: pl.BlockSpec((tm, w), lambda i: (i, 0))
    return pl.pallas_call(
        _merge_kernel,
        grid=(t // tm,),
        in_specs=[
            tok(d), tok(o_sb.shape[1]), tok(o_df.shape[1]), tok(2 * d),
            pl.BlockSpec((None, r, d), lambda i: (i // tiles_per_group, 0, 0)),
            _resident(wos.shape), _resident(wod.shape), _resident(wout.shape),
        ],
        out_specs=tok(d),
        out_shape=jax.ShapeDtypeStruct((t, d), F32),
        compiler_params=_cparams(1),
        name="merge",
    )(x, o_sb, o_df, g, gate, wos, wod, wout)


FFN_CHUNK = 256
PROMPT_FFN_TILE = 1024
PROMPT_PROJ_TILE = 512
PROMPT_MERGE_TILE = 512
PROMPT_ATTN_BLOCK = 256
DECODE_PAGES_PER_STEP = 8


def kernel(x_prompt, x_sample, c_prompt, c_sample, cache_sba_k, cache_sba_v, cache_diff_k, cache_diff_v, page_table, norm_w, w_ada, b_ada, ffn_w_in, ffn_w_out, w_in, w_o_sba, w_o_diff, w_out, diff_lambda, diff_subln_w, final_norm_w):
    batch, seq, d = x_prompt.shape
    n_seq, n_new, _ = x_sample.shape
    depth = norm_w.shape[0]
    n_pool = cache_sba_k.shape[1]
    past = page_table.shape[1] * PAGE_SIZE
    sb = N_SB_HEADS * HEAD_DIM
    df = N_DF_HEADS * 2 * HEAD_DIM
    t_s = n_seq * n_new

    mod = _ada_mod(jnp.concatenate([c_prompt, c_sample], axis=0), w_ada, b_ada)
    mod = mod.reshape(depth, batch + n_seq, N_SUBLAYERS, 3, d)

    sb_kt = jnp.transpose(cache_sba_k, (0, 1, 3, 4, 2)).reshape(depth, n_pool, sb, PAGE_SIZE)
    sb_vt = jnp.transpose(cache_sba_v, (0, 1, 3, 4, 2)).reshape(depth, n_pool, sb, PAGE_SIZE)
    df_k = cache_diff_k.reshape(depth, n_pool, PAGE_SIZE * N_DF_HEADS, 2 * HEAD_DIM)
    df_v = cache_diff_v.reshape(depth, n_pool, PAGE_SIZE * N_DF_HEADS, 2 * HEAD_DIM)

    cos_p, sin_p = _rope_tables(jnp.arange(seq, dtype=jnp.int32), N_SB_HEADS)
    cos_s, sin_s = _rope_tables(jnp.tile(past + jnp.arange(n_new, dtype=jnp.int32), n_seq), N_SB_HEADS)

    xp = x_prompt.reshape(batch * seq, d)
    xs = x_sample.reshape(t_s, d)
    kv_prompt = None
    kv_sample = []
    for layer in range(depth):
        last = layer == depth - 1
        wi_a, wo_a = _prep_ffn_weights(ffn_w_in[layer, 0], ffn_w_out[layer, 0], FFN_CHUNK)
        wi_b, wo_b = _prep_ffn_weights(ffn_w_in[layer, 1], ffn_w_out[layer, 1], FFN_CHUNK)
        w_b = w_in[layer].astype(BF16)
        wkv_t = w_in[layer][:, sb:3 * sb].T.astype(BF16)
        wos, wod, wout = w_o_sba[layer].astype(BF16), w_o_diff[layer].astype(BF16), w_out[layer].astype(BF16)
        mp = mod[layer, :batch]
        ms = jnp.repeat(mod[layer, batch:], n_new, axis=0)

        pm = lambda sub, kind: mp[:, sub, kind][:, None, :]
        xp = _ffn(xp, norm_w[layer, 0], pm(0, 0), pm(0, 1), pm(0, 2), wi_a, wo_a, final_norm_w, tm=min(PROMPT_FFN_TILE, seq), final_norm=False)
        q_sb, kt, vt, q_df, kdf, vdf, g = _in_proj_prompt(
            xp, norm_w[layer, 1], pm(1, 0), pm(1, 1), cos_p, sin_p, w_b, wkv_t, kv_prompt, layer,
            depth=depth, batch=batch, seq=seq, tm=min(PROMPT_PROJ_TILE, seq))
        kv_prompt = (kt, vt, kdf, vdf)
        o_sb, o_df = _prompt_attn(q_sb, kt, vt, q_df, kdf, vdf, diff_lambda, diff_subln_w, layer, batch=batch, seq=seq, blk=PROMPT_ATTN_BLOCK)
        xp = _merge(xp, o_sb, o_df, g, pm(1, 2), wos, wod, wout, tm=min(PROMPT_MERGE_TILE, seq))
        xp = _ffn(xp, norm_w[layer, 2], pm(2, 0), pm(2, 1), pm(2, 2), wi_b, wo_b, final_norm_w, tm=min(PROMPT_FFN_TILE, seq), final_norm=last)

        sm = lambda sub, kind: ms[:, sub, kind][None]
        xs = _ffn(xs, norm_w[layer, 0], sm(0, 0), sm(0, 1), sm(0, 2), wi_a, wo_a, final_norm_w, tm=t_s, final_norm=False)
        q_sb, k_sb, v_sb, q_df, k_df, v_df, g = _in_proj_sample(xs, norm_w[layer, 1], ms[:, 1, 0], ms[:, 1, 1], cos_s, sin_s, w_b)
        kv_sample.append((k_sb, v_sb, k_df, v_df))
        o_sb = _decode_sb(page_table, q_sb, k_sb, v_sb, sb_kt, sb_vt, layer, n_new=n_new)
        o_df = _decode_diff(
            page_table, q_df, k_df.reshape(n_seq, N_DF_HEADS * n_new, 2 * HEAD_DIM), v_df.reshape(n_seq, N_DF_HEADS * n_new, 2 * HEAD_DIM),
            df_k, df_v, diff_lambda, diff_subln_w, layer, n_new=n_new, pages_per_step=DECODE_PAGES_PER_STEP)
        xs = _merge(xs, o_sb, o_df, g, sm(1, 2), wos, wod, wout, tm=t_s)
        xs = _ffn(xs, norm_w[layer, 2], sm(2, 0), sm(2, 1), sm(2, 2), wi_b, wo_b, final_norm_w, tm=t_s, final_norm=last)

    kt, vt, kdf, vdf = kv_prompt
    stack = lambda n, shape: jnp.stack([kv[n] for kv in kv_sample]).reshape(shape)
    return (
        xp.reshape(batch, seq, d),
        xs.reshape(n_seq, n_new, d),
        jnp.transpose(kt.reshape(depth, batch, N_SB_HEADS, HEAD_DIM, seq), (0, 1, 4, 2, 3)),
        jnp.transpose(vt.reshape(depth, batch, N_SB_HEADS, HEAD_DIM, seq), (0, 1, 4, 2, 3)),
        kdf.reshape(depth, batch, seq, N_DF_HEADS, 2 * HEAD_DIM),
        vdf.reshape(depth, batch, seq, N_DF_HEADS, 2 * HEAD_DIM),
        stack(0, (depth, n_seq, n_new, N_SB_HEADS, HEAD_DIM)),
        stack(1, (depth, n_seq, n_new, N_SB_HEADS, HEAD_DIM)),
        stack(2, (depth, n_seq, n_new, N_DF_HEADS, 2 * HEAD_DIM)),
        stack(3, (depth, n_seq, n_new, N_DF_HEADS, 2 * HEAD_DIM)),
    )
```

```python
import functools
import math

import jax
import jax.numpy as jnp
from jax import lax
from jax.experimental import pallas as pl
from jax.experimental.pallas import tpu as pltpu

HEAD_DIM = 64
N_SB_HEADS = 8
N_DF_HEADS = 4
PAGE_SIZE = 128
ROPE_THETA = 10000.0
NORM_EPS = 1e-6
SUBLN_EPS = 1e-5
N_SUBLAYERS = 3
SB_LOG_WEIGHT_FLOOR = -104.0
V7X_VMEM_LIMIT_BYTES = 56 * 1024 * 1024
MASK_VALUE = -1e30

F32 = jnp.float32
BF16 = jnp.bfloat16


def _cparams(n_axes):
    return pltpu.CompilerParams(
        dimension_semantics=("arbitrary",) * n_axes,
        vmem_limit_bytes=V7X_VMEM_LIMIT_BYTES,
    )


def _resident(shape):
    nd = len(shape)
    return pl.BlockSpec(shape, lambda *_: (0,) * nd, pipeline_mode=pl.Buffered(1))


def _dot(a, b):
    return jnp.dot(a, b, preferred_element_type=F32)


def _dot_nt(a, b):
    return lax.dot_general(a, b, (((1,), (1,)), ((), ())), preferred_element_type=F32)


def _dot_tn(a, b):
    return lax.dot_general(a, b, (((0,), (0,)), ((), ())), preferred_element_type=F32)


def _split_bf16(x):
    hi = x.astype(BF16)
    lo = (x - hi.astype(F32)).astype(BF16)
    return hi, lo


def _rms(x, eps):
    return x * lax.rsqrt(jnp.mean(x * x, axis=-1, keepdims=True) + eps)


def _sigmoid(x):
    return 1.0 / (1.0 + jnp.exp(-x))


def _ada_mod_kernel(c_ref, w_ref, b_ref, o_ref):
    c = c_ref[...]
    a_hi, a_lo = _split_bf16(c * _sigmoid(c))
    w_hi, w_lo = _split_bf16(w_ref[...])
    o_ref[...] = _dot(a_hi, w_hi) + (_dot(a_hi, w_lo) + _dot(a_lo, w_hi)) + b_ref[...]


def _ada_mod(c_all, w_ada, b_ada):
    n, d = c_all.shape
    depth, _, width = w_ada.shape
    tn = 1536
    assert width % tn == 0
    return pl.pallas_call(
        _ada_mod_kernel,
        grid=(depth, width // tn),
        in_specs=[
            pl.BlockSpec((n, d), lambda l, j: (0, 0)),
            pl.BlockSpec((None, d, tn), lambda l, j: (l, 0, j)),
            pl.BlockSpec((None, 1, tn), lambda l, j: (l, 0, j)),
        ],
        out_specs=pl.BlockSpec((None, n, tn), lambda l, j: (l, 0, j)),
        out_shape=jax.ShapeDtypeStruct((depth, n, width), F32),
        compiler_params=_cparams(2),
        name="ada_mod",
    )(c_all, w_ada, b_ada.reshape(depth, 1, width))


def _ffn_kernel(x_ref, nw_ref, shift_ref, scale_ref, gate_ref, wi_ref, wo_ref, fw_ref, o_ref, *, n_chunks, final_norm):
    x = x_ref[...]
    h = _rms(x, NORM_EPS) * nw_ref[...]
    hb = (h * (1.0 + scale_ref[...]) + shift_ref[...]).astype(BF16)
    acc = None
    for j in range(n_chunks):
        g = _dot(hb, wi_ref[0, j])
        u = _dot(hb, wi_ref[1, j])
        a = (g * _sigmoid(g) * u).astype(BF16)
        part = _dot(a, wo_ref[j])
        acc = part if acc is None else acc + part
    y = x + (0.5 * gate_ref[...]) * acc
    if final_norm:
        y = _rms(y, NORM_EPS) * fw_ref[...]
    o_ref[...] = y


def _ffn(x, norm_w, shift, scale, gate, wi, wo, final_w, *, tm, final_norm):
    t, d = x.shape
    groups, r, _ = shift.shape
    _, n_chunks, _, fc = wi.shape
    assert t % (tm * groups) == 0
    tiles_per_group = t // (tm * groups)
    mod_spec = pl.BlockSpec((None, r, d), lambda i: (i // tiles_per_group, 0, 0))
    row_spec = pl.BlockSpec((1, d), lambda i: (0, 0))
    return pl.pallas_call(
        functools.partial(_ffn_kernel, n_chunks=n_chunks, final_norm=final_norm),
        grid=(t // tm,),
        in_specs=[
            pl.BlockSpec((tm, d), lambda i: (i, 0)),
            row_spec, mod_spec, mod_spec, mod_spec,
            _resident(wi.shape), _resident(wo.shape),
            row_spec,
        ],
        out_specs=pl.BlockSpec((tm, d), lambda i: (i, 0)),
        out_shape=jax.ShapeDtypeStruct((t, d), F32),
        compiler_params=_cparams(1),
        name="ffn",
    )(x, norm_w.reshape(1, d), shift, scale, gate, wi, wo, final_w.reshape(1, d))


def _prep_ffn_weights(w_in, w_out, fc):
    d, two_f = w_in.shape
    f = two_f // 2
    assert f % fc == 0
    n_chunks = f // fc
    wi = w_in.reshape(d, 2, n_chunks, fc).transpose(1, 2, 0, 3).astype(BF16)
    wo = w_out.reshape(n_chunks, fc, d).astype(BF16)
    return wi, wo


def _rope(y, cos, sin_signed):
    width = y.shape[-1]
    half = HEAD_DIM // 2
    lane = lax.broadcasted_iota(jnp.int32, y.shape, 1)
    partner = jnp.where(lane % HEAD_DIM < half, pltpu.roll(y, width - half, 1), pltpu.roll(y, half, 1))
    return y * cos + partner * sin_signed


def _rope_t(yt, cos_t, sin_signed_t):
    n_rows = yt.shape[0]
    half = HEAD_DIM // 2
    r = lax.broadcasted_iota(jnp.int32, yt.shape, 0)
    partner = jnp.where(r % HEAD_DIM < half, pltpu.roll(yt, n_rows - half, 0), pltpu.roll(yt, half, 0))
    return yt * cos_t + partner * sin_signed_t


def _adaln_bf16(x_ref, nw_ref, shift_ref, scale_ref):
    h = _rms(x_ref[...], NORM_EPS) * nw_ref[...]
    return (h * (1.0 + scale_ref[...]) + shift_ref[...]).astype(BF16)


def _in_proj_prompt_kernel(x_ref, nw_ref, shift_ref, scale_ref, cos_ref, sin_ref, cos_t_ref, sin_t_ref, wt_ref, wn_ref,
                           qsb_t_ref, qdf_t_ref, ktok_ref, ksb_t_ref, vsb_t_ref, kdf_ref, vdf_ref, g_ref):
    sb = N_SB_HEADS * HEAD_DIM
    df = N_DF_HEADS * 2 * HEAD_DIM
    tm = x_ref.shape[0]
    scale = HEAD_DIM ** -0.5
    hb = _adaln_bf16(x_ref, nw_ref, shift_ref, scale_ref)
    yt = _dot_nt(wt_ref[...], hb)
    qsb_t_ref[...] = (yt[0:sb] * scale).astype(BF16)
    ksb_t_ref[...] = yt[sb:2 * sb]
    vsb_t_ref[...] = yt[2 * sb:3 * sb]
    qdf_t_ref[...] = (_rope_t(yt[3 * sb:3 * sb + df], cos_t_ref[...], sin_t_ref[...]) * scale).astype(BF16)
    yn = _dot(hb, wn_ref[...])
    ktok_ref[...] = yn[:, 0:sb].astype(BF16)
    k_df = _rope(yn[:, sb:sb + df], cos_ref[...], sin_ref[...])
    v_df = yn[:, sb + df:sb + 2 * df]
    g_ref[...] = yn[:, sb + 2 * df:]
    width = 2 * HEAD_DIM
    for head in range(N_DF_HEADS):
        kdf_ref[pl.ds(head, tm, stride=N_DF_HEADS), :] = k_df[:, head * width:(head + 1) * width]
        vdf_ref[pl.ds(head, tm, stride=N_DF_HEADS), :] = v_df[:, head * width:(head + 1) * width]


def _in_proj_sample_kernel(x_ref, nw_ref, shift_ref, scale_ref, cos_ref, sin_ref, w_ref,
                           qsb_ref, ksb_ref, vsb_ref, qdf_ref, kdf_ref, vdf_ref, g_ref):
    sb = N_SB_HEADS * HEAD_DIM
    df = N_DF_HEADS * 2 * HEAD_DIM
    hb = _adaln_bf16(x_ref, nw_ref, shift_ref, scale_ref)
    cos = cos_ref[...]
    sin = sin_ref[...]
    qsb_ref[...] = _dot(hb, w_ref[:, 0:sb])
    ksb_ref[...] = _dot(hb, w_ref[:, sb:2 * sb])
    vsb_ref[...] = _dot(hb, w_ref[:, 2 * sb:3 * sb])
    qdf_ref[...] = _rope(_dot(hb, w_ref[:, 3 * sb:3 * sb + df]), cos, sin)
    kdf_ref[...] = _rope(_dot(hb, w_ref[:, 3 * sb + df:3 * sb + 2 * df]), cos, sin)
    vdf_ref[...] = _dot(hb, w_ref[:, 3 * sb + 2 * df:3 * sb + 3 * df])
    g_ref[...] = _dot(hb, w_ref[:, 3 * sb + 3 * df:])


def _in_proj_prompt(x, norm_w, shift, scale, cos, sin, cos_t, sin_t, w_t, w_n, kv_prev, layer, *, depth, batch, seq, tm):
    t, d = x.shape
    sb = N_SB_HEADS * HEAD_DIM
    df = N_DF_HEADS * 2 * HEAD_DIM
    tiles_per_seq = seq // tm
    tok = lambda i: (i, 0)
    const = lambda i: (0, 0)
    mod_spec = pl.BlockSpec((None, 1, d), lambda i: (i // tiles_per_seq, 0, 0))
    tab_spec = pl.BlockSpec((tm, df), lambda i: (i % tiles_per_seq, 0))
    tab_t_spec = pl.BlockSpec((df, tm), lambda i: (0, i % tiles_per_seq))
    in_specs = [
        pl.BlockSpec((tm, d), tok), pl.BlockSpec((1, d), const), mod_spec, mod_spec, tab_spec, tab_spec, tab_t_spec, tab_t_spec,
        _resident(w_t.shape), _resident(w_n.shape),
    ]
    args = [x, norm_w.reshape(1, d), shift, scale, cos, sin, cos_t, sin_t, w_t, w_n]
    n_in = len(args)
    aliases = {}
    if kv_prev is not None:
        for n, buf in enumerate(kv_prev):
            in_specs.append(pl.BlockSpec(memory_space=pl.ANY))
            aliases[len(args)] = (3, 4, 5, 6)[n]
            args.append(buf)
    n_alias = len(aliases)
    qt_spec = pl.BlockSpec((None, sb, tm), lambda i: (i // tiles_per_seq, 0, i % tiles_per_seq))
    kt_spec = pl.BlockSpec((None, None, sb, tm), lambda i: (layer, i // tiles_per_seq, 0, i % tiles_per_seq))
    il_spec = pl.BlockSpec((None, None, N_DF_HEADS * tm, 2 * HEAD_DIM), lambda i: (layer, i // tiles_per_seq, i % tiles_per_seq, 0))
    qt_shape = jax.ShapeDtypeStruct((batch, sb, seq), BF16)
    kt_shape = jax.ShapeDtypeStruct((depth, batch, sb, seq), F32)
    il_shape = jax.ShapeDtypeStruct((depth, batch, N_DF_HEADS * seq, 2 * HEAD_DIM), F32)

    def body(*refs):
        _in_proj_prompt_kernel(*refs[:n_in], *refs[n_in + n_alias:])

    return pl.pallas_call(
        body,
        grid=(t // tm,),
        in_specs=in_specs,
        out_specs=[qt_spec, qt_spec, pl.BlockSpec((tm, sb), tok), kt_spec, kt_spec, il_spec, il_spec, pl.BlockSpec((tm, 2 * d), tok)],
        out_shape=[qt_shape, qt_shape, jax.ShapeDtypeStruct((t, sb), BF16), kt_shape, kt_shape, il_shape, il_shape,
                   jax.ShapeDtypeStruct((t, 2 * d), F32)],
        input_output_aliases=aliases,
        compiler_params=_cparams(1),
        name="in_proj_prompt",
    )(*args)


def _in_proj_sample(x, norm_w, shift, scale, cos, sin, w_b):
    t, d = x.shape
    sb = N_SB_HEADS * HEAD_DIM
    df = N_DF_HEADS * 2 * HEAD_DIM
    full = lambda shape: pl.BlockSpec(shape, lambda i: (0,) * len(shape))
    widths = (sb, sb, sb, df, df, df, 2 * d)
    return pl.pallas_call(
        _in_proj_sample_kernel,
        grid=(1,),
        in_specs=[full((t, d)), full((1, d)), full((t, d)), full((t, d)), full((t, df)), full((t, df)), full(w_b.shape)],
        out_specs=[full((t, w)) for w in widths],
        out_shape=[jax.ShapeDtypeStruct((t, w), F32) for w in widths],
        compiler_params=_cparams(1),
        name="in_proj_sample",
    )(x, norm_w.reshape(1, d), shift, scale, cos, sin, w_b)


def _rope_tables(pos, n_heads):
    half = HEAD_DIM // 2
    inv = 1.0 / (ROPE_THETA ** (jnp.arange(half, dtype=F32) / half))
    ang = pos.astype(F32)[:, None] * inv[None, :]
    cos, sin = jnp.cos(ang), jnp.sin(ang)
    cos = jnp.tile(jnp.concatenate([cos, cos], axis=1), (1, n_heads))
    sin = jnp.tile(jnp.concatenate([-sin, sin], axis=1), (1, n_heads))
    return cos, sin


def _later_column_sum_matrix(n):
    row = lax.broadcasted_iota(jnp.int32, (2 * n, n), 0) % n
    col = lax.broadcasted_iota(jnp.int32, (2 * n, n), 1)
    return jnp.where(row > col, 1.0, 0.0).astype(BF16)


def _sb_block(z, allowed, carry, tri2):
    log_beta = jnp.minimum(z, 0.0) - jnp.log(1.0 + jnp.exp(-jnp.abs(z)))
    log_keep = log_beta - z
    if allowed is not None:
        log_keep = jnp.where(allowed, log_keep, 0.0)
    hi, lo = _split_bf16(log_keep)
    log_between = _dot(jnp.concatenate([hi, lo], axis=1), tri2)
    w = jnp.exp(log_beta + log_between + carry)
    if allowed is not None:
        w = jnp.where(allowed, w, 0.0)
    return w, carry + jnp.sum(log_keep, axis=-1, keepdims=True)


def _softmax_block(s, v_b, state):
    m, l, acc = state
    m_new = jnp.maximum(m, jnp.max(s, axis=-1, keepdims=True))
    alpha = jnp.exp(m - m_new)
    p = jnp.exp(s - m_new)
    l = alpha * l + jnp.sum(p, axis=-1, keepdims=True)
    acc = alpha * acc + _dot(p.astype(BF16), v_b)
    return m_new, l, acc


def _diff_lambda(dl_ref, lam_init):
    lp = dl_ref[...]
    a = jnp.sum(lp[0:1] * lp[1:2], axis=-1, keepdims=True)
    b = jnp.sum(lp[2:3] * lp[3:4], axis=-1, keepdims=True)
    return jnp.exp(a) - jnp.exp(b) + lam_init


def _sub_norm(o, subw, lam_init):
    return _rms(o, SUBLN_EPS) * subw * (1.0 - lam_init)


def _lam_init(layer):
    return 0.8 - 0.6 * math.exp(-0.3 * layer)


def _later_row_sum_matrix(n):
    row = lax.broadcasted_iota(jnp.int32, (n, 2 * n), 0)
    col = lax.broadcasted_iota(jnp.int32, (n, 2 * n), 1) % n
    return jnp.where(col > row, 1.0, 0.0).astype(BF16)


def _sb_block_t(zt, allowed, carry, u2):
    log_beta = jnp.minimum(zt, 0.0) - jnp.log(1.0 + jnp.exp(-jnp.abs(zt)))
    log_keep = log_beta - zt
    if allowed is not None:
        log_keep = jnp.where(allowed, log_keep, 0.0)
    hi, lo = _split_bf16(log_keep)
    log_between = _dot(u2, jnp.concatenate([hi, lo], axis=0))
    w = jnp.exp(log_beta + log_between + carry)
    if allowed is not None:
        w = jnp.where(allowed, w, 0.0)
    return w, carry + jnp.sum(log_keep, axis=0, keepdims=True)


def _split_cols(qt, upper):
    q = qt.astype(F32)
    return jnp.concatenate([jnp.where(upper, q, 0.0), jnp.where(upper, 0.0, q)], axis=1).astype(BF16)


def _prompt_sb_t_kernel(qt_ref, k_ref, vt_ref, o_ref, *, blk):
    seq = k_ref.shape[0]
    n_blk = seq // blk
    width = 2 * HEAD_DIM
    pairs = N_SB_HEADS // 2
    cols = pairs * 2 * blk
    u2 = _later_row_sum_matrix(blk)
    upper = lax.broadcasted_iota(jnp.int32, (width, blk), 0) < HEAD_DIM

    def q_block(iq, carry):
        q0 = pl.multiple_of(iq * blk, blk)
        qs = [_split_cols(qt_ref[p * width:(p + 1) * width, pl.ds(q0, blk)], upper) for p in range(pairs)]

        def step(jk, c, accs, masked):
            k0 = pl.multiple_of(jk * blk, blk)
            zt = jnp.concatenate([_dot(k_ref[pl.ds(k0, blk), p * width:(p + 1) * width], qs[p]) for p in range(pairs)], axis=1)
            allowed = None
            if masked:
                key = lax.broadcasted_iota(jnp.int32, (blk, cols), 0)
                query = lax.broadcasted_iota(jnp.int32, (blk, cols), 1) % blk
                allowed = key < query
            w, c = _sb_block_t(zt, allowed, c, u2)
            wb = w.astype(BF16)
            accs = tuple(
                accs[p] + _dot(vt_ref[p * width:(p + 1) * width, pl.ds(k0, blk)].astype(BF16), wb[:, p * 2 * blk:(p + 1) * 2 * blk])
                for p in range(pairs))
            return c, accs

        zero_acc = tuple(jnp.zeros((width, 2 * blk), F32) for _ in range(pairs))
        c, accs = step(iq, jnp.zeros((1, cols), F32), zero_acc, True)

        def cond(st):
            jk, c, _ = st
            return jnp.logical_and(jk >= 0, jnp.max(c) >= SB_LOG_WEIGHT_FLOOR)

        def body(st):
            jk, c, accs = st
            c, accs = step(jk, c, accs, False)
            return jk - 1, c, accs

        _, _, accs = lax.while_loop(cond, body, (iq - 1, c, accs))
        for p in range(pairs):
            o_ref[p * width:(p + 1) * width, pl.ds(q0, blk)] = jnp.where(upper, accs[p][:, :blk], accs[p][:, blk:])
        return carry

    lax.fori_loop(0, n_blk, q_block, 0)


def _prompt_sb_t(qt, k_tok, vt, layer, *, batch, seq, blk):
    sb = qt.shape[1]
    t_spec = pl.BlockSpec((None, sb, seq), lambda b: (b, 0, 0))
    return pl.pallas_call(
        functools.partial(_prompt_sb_t_kernel, blk=blk),
        grid=(batch,),
        in_specs=[t_spec, pl.BlockSpec((seq, sb), lambda b: (b, 0)), pl.BlockSpec((None, None, sb, seq), lambda b: (layer, b, 0, 0))],
        out_specs=t_spec,
        out_shape=jax.ShapeDtypeStruct((batch, sb, seq), F32),
        compiler_params=_cparams(1),
        name="prompt_sb",
    )(qt, k_tok, vt)


def _prompt_diff_t_kernel(qt_ref, k_ref, v_ref, dl_ref, subw_ref, o_ref, *, blk, lam_init):
    seq = qt_ref.shape[1]
    n_blk = seq // blk
    width = 2 * HEAD_DIM
    cols = N_DF_HEADS * 2 * blk
    lam = _diff_lambda(dl_ref, lam_init)
    subw = subw_ref[...]
    upper = lax.broadcasted_iota(jnp.int32, (width, blk), 0) < HEAD_DIM

    def q_block(iq, carry):
        q0 = pl.multiple_of(iq * blk, blk)
        qs = [_split_cols(qt_ref[h * width:(h + 1) * width, pl.ds(q0, blk)], upper) for h in range(N_DF_HEADS)]

        def step(jk, state, masked):
            m, l, accs = state
            base = jk * (blk * N_DF_HEADS)
            ks = [k_ref[pl.ds(base + h, blk, stride=N_DF_HEADS), :].astype(BF16) for h in range(N_DF_HEADS)]
            vs = [v_ref[pl.ds(base + h, blk, stride=N_DF_HEADS), :].astype(BF16) for h in range(N_DF_HEADS)]
            st = jnp.concatenate([_dot(ks[h], qs[h]) for h in range(N_DF_HEADS)], axis=1)
            if masked:
                key = lax.broadcasted_iota(jnp.int32, (blk, cols), 0)
                query = lax.broadcasted_iota(jnp.int32, (blk, cols), 1) % blk
                st = jnp.where(key <= query, st, MASK_VALUE)
            m_new = jnp.maximum(m, jnp.max(st, axis=0, keepdims=True))
            alpha = jnp.exp(m - m_new)
            p = jnp.exp(st - m_new)
            l = alpha * l + jnp.sum(p, axis=0, keepdims=True)
            pb = p.astype(BF16)
            accs = tuple(
                alpha[:, h * 2 * blk:(h + 1) * 2 * blk] * accs[h] + _dot_tn(vs[h], pb[:, h * 2 * blk:(h + 1) * 2 * blk])
                for h in range(N_DF_HEADS))
            return m_new, l, accs

        state = (jnp.full((1, cols), MASK_VALUE, F32), jnp.zeros((1, cols), F32),
                 tuple(jnp.zeros((width, 2 * blk), F32) for _ in range(N_DF_HEADS)))
        state = lax.fori_loop(0, iq, lambda jk, st: step(jk, st, False), state)
        _, l, accs = step(iq, state, True)
        for h in range(N_DF_HEADS):
            a = accs[h] / l[:, h * 2 * blk:(h + 1) * 2 * blk]
            o = a[:, :blk] - lam * a[:, blk:]
            o = o * lax.rsqrt(jnp.mean(o * o, axis=0, keepdims=True) + SUBLN_EPS)
            o_ref[h * width:(h + 1) * width, pl.ds(q0, blk)] = o * subw * (1.0 - lam_init)
        return carry

    lax.fori_loop(0, n_blk, q_block, 0)


def _prompt_diff_t(qt, kdf, vdf, diff_lambda, subln_w, layer, *, batch, seq, blk):
    df = qt.shape[1]
    width = 2 * HEAD_DIM
    t_spec = pl.BlockSpec((None, df, seq), lambda b: (b, 0, 0))
    il_spec = pl.BlockSpec((None, None, N_DF_HEADS * seq, width), lambda b: (layer, b, 0, 0))
    return pl.pallas_call(
        functools.partial(_prompt_diff_t_kernel, blk=blk, lam_init=_lam_init(layer)),
        grid=(batch,),
        in_specs=[
            t_spec, il_spec, il_spec,
            pl.BlockSpec((None, 4, HEAD_DIM), lambda b: (layer, 0, 0)),
            pl.BlockSpec((None, width, 1), lambda b: (layer, 0, 0)),
        ],
        out_specs=t_spec,
        out_shape=jax.ShapeDtypeStruct((batch, df, seq), F32),
        compiler_params=_cparams(1),
        name="prompt_diff",
    )(qt, kdf, vdf, diff_lambda, subln_w.reshape(subln_w.shape[0], width, 1))


def _decode_sb_kernel(pt_ref, q_ref, kn_ref, vn_ref, kc_ref, vc_ref, o_ref, kbuf, vbuf, sem, *, layer, n_pages):
    b = pl.program_id(0)
    n_new, sb = q_ref.shape
    rows = N_SB_HEADS * n_new

    def page_copies(j, slot):
        page = pt_ref[b, j]
        return (pltpu.make_async_copy(kc_ref.at[layer, page], kbuf.at[slot], sem.at[0, slot]),
                pltpu.make_async_copy(vc_ref.at[layer, page], vbuf.at[slot], sem.at[1, slot]))

    def start_page(j, slot):
        for cp in page_copies(j, slot):
            cp.start()

    def wait_page(j, slot):
        for cp in page_copies(j, slot):
            cp.wait()

    start_page(n_pages - 1, (n_pages - 1) % 2)

    tri = _later_column_sum_matrix(PAGE_SIZE)
    row_head = lax.broadcasted_iota(jnp.int32, (rows, sb), 0) // n_new
    lane_head = lax.broadcasted_iota(jnp.int32, (rows, sb), 1) // HEAD_DIM
    own_head = row_head == lane_head
    q = jnp.tile(q_ref[...] * (HEAD_DIM ** -0.5), (N_SB_HEADS, 1))
    qbd = jnp.where(own_head, q, 0.0).astype(BF16)

    pad = jnp.zeros((PAGE_SIZE - n_new, sb), F32)
    k_new = jnp.concatenate([kn_ref[...], pad], axis=0).astype(BF16)
    v_new = jnp.concatenate([vn_ref[...], pad], axis=0).astype(BF16)
    query = lax.broadcasted_iota(jnp.int32, (rows, PAGE_SIZE), 0) % n_new
    col = lax.broadcasted_iota(jnp.int32, (rows, PAGE_SIZE), 1)
    w, c = _sb_block(_dot_nt(qbd, k_new), col < query, jnp.zeros((rows, 1), F32), tri)
    acc = _dot(w.astype(BF16), v_new)

    def cond(st):
        j, c, _ = st
        return jnp.logical_and(j >= 0, jnp.max(c) >= SB_LOG_WEIGHT_FLOOR)

    def body(st):
        j, c, acc = st
        slot = j % 2
        wait_page(j, slot)

        @pl.when(j > 0)
        def _():
            start_page(j - 1, 1 - slot)

        w, c = _sb_block(_dot(qbd, kbuf[slot].astype(BF16)), None, c, tri)
        return j - 1, c, acc + _dot_nt(w.astype(BF16), vbuf[slot].astype(BF16))

    j_end, _, acc = lax.while_loop(cond, body, (n_pages - 1, c, acc))

    @pl.when(j_end >= 0)
    def _():
        wait_page(j_end, j_end % 2)

    acc = jnp.where(own_head, acc, 0.0)
    out = acc[0:n_new]
    for head in range(1, N_SB_HEADS):
        out = out + acc[head * n_new:(head + 1) * n_new]
    o_ref[...] = out


def _decode_sb(page_table, q, k_new, v_new, cache_kt, cache_vt, layer, *, n_new):
    t, sb = q.shape
    n_seq, n_pages = page_table.shape
    tok = pl.BlockSpec((n_new, sb), lambda b, pt: (b, 0))
    any_spec = pl.BlockSpec(memory_space=pl.ANY)
    return pl.pallas_call(
        functools.partial(_decode_sb_kernel, layer=layer, n_pages=n_pages),
        grid_spec=pltpu.PrefetchScalarGridSpec(
            num_scalar_prefetch=1,
            grid=(n_seq,),
            in_specs=[tok, tok, tok, any_spec, any_spec],
            out_specs=tok,
            scratch_shapes=[
                pltpu.VMEM((2, sb, PAGE_SIZE), F32),
                pltpu.VMEM((2, sb, PAGE_SIZE), F32),
                pltpu.SemaphoreType.DMA((2, 2)),
            ],
        ),
        out_shape=jax.ShapeDtypeStruct((t, sb), F32),
        compiler_params=_cparams(1),
        name="decode_sb",
    )(page_table, q, k_new, v_new, cache_kt, cache_vt)


def _decode_diff_kernel(*refs, layer, pages_per_step, lam_init):
    (pt_ref, q_ref, kn_ref, vn_ref, kc_ref, vc_ref, dl_ref, subw_ref, o_ref,
     kbuf, vbuf, sem, m_ref, l_ref, acc_ref, qall_ref, bias_ref) = refs
    seq_id = pl.program_id(0)
    step = pl.program_id(1)
    n_steps = pl.num_programs(1)
    n_new = q_ref.shape[0]
    width = 2 * HEAD_DIM
    rows = N_DF_HEADS * 2 * n_new
    page_rows = PAGE_SIZE * N_DF_HEADS

    linear = seq_id * n_steps + step
    slot = linear % 2

    def step_copies(s_id, st, sl):
        copies = []
        for r in range(pages_per_step):
            page = pt_ref[s_id, st * pages_per_step + r]
            copies.append(pltpu.make_async_copy(kc_ref.at[layer, page], kbuf.at[sl, r], sem.at[0, sl]))
            copies.append(pltpu.make_async_copy(vc_ref.at[layer, page], vbuf.at[sl, r], sem.at[1, sl]))
        return copies

    @pl.when(linear == 0)
    def _():
        for cp in step_copies(seq_id, step, slot):
            cp.start()

    @pl.when(linear + 1 < pl.num_programs(0) * n_steps)
    def _():
        nxt = linear + 1
        for cp in step_copies(nxt // n_steps, nxt % n_steps, 1 - slot):
            cp.start()

    for cp in step_copies(seq_id, step, slot):
        cp.wait()

    @pl.when(step == 0)
    def _():
        m_ref[...] = jnp.full(m_ref.shape, MASK_VALUE, F32)
        l_ref[...] = jnp.zeros(l_ref.shape, F32)
        acc_ref[...] = jnp.zeros(acc_ref.shape, F32)
        q = q_ref[...] * (HEAD_DIM ** -0.5)
        first_half = lax.broadcasted_iota(jnp.int32, (n_new, width), 1) < HEAD_DIM
        pieces = []
        for head in range(N_DF_HEADS):
            qh = q[:, head * width:(head + 1) * width]
            pieces += [jnp.where(first_half, qh, 0.0), jnp.where(first_half, 0.0, qh)]
        qall_ref[...] = jnp.concatenate(pieces, axis=0).astype(BF16)
        row_head = lax.broadcasted_iota(jnp.int32, (rows, page_rows), 0) // (2 * n_new)
        col_head = lax.broadcasted_iota(jnp.int32, (rows, page_rows), 1) % N_DF_HEADS
        bias_ref[...] = jnp.where(row_head == col_head, 0.0, MASK_VALUE)

    qall = qall_ref[...]
    bias = bias_ref[...]
    s = jnp.concatenate([_dot_nt(qall, kbuf[slot, r].astype(BF16)) + bias for r in range(pages_per_step)], axis=1)
    m = m_ref[...]
    m_new = jnp.maximum(m, jnp.max(s, axis=-1, keepdims=True))
    alpha = jnp.exp(m - m_new)
    p = jnp.exp(s - m_new).astype(BF16)
    l_ref[...] = alpha * l_ref[...] + jnp.sum(p.astype(F32), axis=-1, keepdims=True)
    pv = _dot(p[:, 0:page_rows], vbuf[slot, 0].astype(BF16))
    for r in range(1, pages_per_step):
        pv = pv + _dot(p[:, r * page_rows:(r + 1) * page_rows], vbuf[slot, r].astype(BF16))
    acc_ref[...] = alpha * acc_ref[...] + pv
    m_ref[...] = m_new

    @pl.when(step == pl.num_programs(1) - 1)
    def _():
        pad = jnp.zeros((PAGE_SIZE - N_DF_HEADS * n_new, width), F32)
        k_new = jnp.concatenate([kn_ref[...], pad], axis=0).astype(BF16)
        v_new = jnp.concatenate([vn_ref[...], pad], axis=0).astype(BF16)
        r_idx = lax.broadcasted_iota(jnp.int32, (rows, PAGE_SIZE), 0)
        c_idx = lax.broadcasted_iota(jnp.int32, (rows, PAGE_SIZE), 1)
        allowed = jnp.logical_and(c_idx % N_DF_HEADS == r_idx // (2 * n_new), c_idx // N_DF_HEADS <= r_idx % n_new)
        s_new = jnp.where(allowed, _dot_nt(qall, k_new), MASK_VALUE)
        _, l, acc = _softmax_block(s_new, v_new, (m_ref[...], l_ref[...], acc_ref[...]))
        o = acc / l
        lam = _diff_lambda(dl_ref, lam_init)
        subw = subw_ref[...]
        for head in range(N_DF_HEADS):
            o1 = o[(2 * head) * n_new:(2 * head + 1) * n_new]
            o2 = o[(2 * head + 1) * n_new:(2 * head + 2) * n_new]
            o_ref[:, head * width:(head + 1) * width] = _sub_norm(o1 - lam * o2, subw, lam_init)


def _decode_diff(page_table, q, k_new, v_new, cache_k, cache_v, diff_lambda, subln_w, layer, *, n_new, pages_per_step):
    t, df = q.shape
    n_seq, n_pages = page_table.shape
    assert n_pages % pages_per_step == 0
    width = 2 * HEAD_DIM
    rows = N_DF_HEADS * 2 * n_new
    page_rows = PAGE_SIZE * N_DF_HEADS
    tok = pl.BlockSpec((n_new, df), lambda b, c, pt: (b, 0))
    new_spec = pl.BlockSpec((None, N_DF_HEADS * n_new, width), lambda b, c, pt: (b, 0, 0))
    any_spec = pl.BlockSpec(memory_space=pl.ANY)
    return pl.pallas_call(
        functools.partial(_decode_diff_kernel, layer=layer, pages_per_step=pages_per_step, lam_init=_lam_init(layer)),
        grid_spec=pltpu.PrefetchScalarGridSpec(
            num_scalar_prefetch=1,
            grid=(n_seq, n_pages // pages_per_step),
            in_specs=[
                tok, new_spec, new_spec, any_spec, any_spec,
                pl.BlockSpec((None, 4, HEAD_DIM), lambda b, c, pt: (layer, 0, 0)),
                pl.BlockSpec((None, 1, width), lambda b, c, pt: (layer, 0, 0)),
            ],
            out_specs=tok,
            scratch_shapes=[
                pltpu.VMEM((2, pages_per_step, page_rows, width), F32),
                pltpu.VMEM((2, pages_per_step, page_rows, width), F32),
                pltpu.SemaphoreType.DMA((2, 2)),
                pltpu.VMEM((rows, 1), F32), pltpu.VMEM((rows, 1), F32), pltpu.VMEM((rows, width), F32),
                pltpu.VMEM((rows, width), BF16), pltpu.VMEM((rows, page_rows), F32),
            ],
        ),
        out_shape=jax.ShapeDtypeStruct((t, df), F32),
        compiler_params=_cparams(2),
        name="decode_diff",
    )(page_table, q, k_new, v_new, cache_k, cache_v, diff_lambda, subln_w.reshape(subln_w.shape[0], 1, width))


def _merge_kernel(x_ref, osb_ref, odf_ref, g_ref, gate_ref, wos_ref, wod_ref, wout_ref, o_ref, *, transposed):
    d = x_ref.shape[1]
    g = g_ref[...]
    mix = _dot_tn if transposed else _dot
    merged = (_sigmoid(g[:, :d]) * mix(osb_ref[...].astype(BF16), wos_ref[...])
              + _sigmoid(g[:, d:]) * mix(odf_ref[...].astype(BF16), wod_ref[...]))
    o_ref[...] = x_ref[...] + gate_ref[...] * _dot(merged.astype(BF16), wout_ref[...])


def _merge(x, o_sb, o_df, g, gate, wos, wod, wout, *, tm):
    t, d = x.shape
    groups, r, _ = gate.shape
    tiles_per_group = t // (tm * groups)
    tok = lambda w: pl.BlockSpec((tm, w), lambda i: (i, 0))
    transposed = o_sb.ndim == 3
    if transposed:
        tiles_per_seq = o_sb.shape[2] // tm
        mixer = lambda o: pl.BlockSpec((None, o.shape[1], tm), lambda i: (i // tiles_per_seq, 0, i % tiles_per_seq))
    else:
        mixer = lambda o: tok(o.shape[1])
    return pl.pallas_call(
        functools.partial(_merge_kernel, transposed=transposed),
        grid=(t // tm,),
        in_specs=[
            tok(d), mixer(o_sb), mixer(o_df), tok(2 * d),
            pl.BlockSpec((None, r, d), lambda i: (i // tiles_per_group, 0, 0)),
            _resident(wos.shape), _resident(wod.shape), _resident(wout.shape),
        ],
        out_specs=tok(d),
        out_shape=jax.ShapeDtypeStruct((t, d), F32),
        compiler_params=_cparams(1),
        name="merge",
    )(x, o_sb, o_df, g, gate, wos, wod, wout)


FFN_CHUNK = 256
PROMPT_FFN_TILE = 1024
PROMPT_PROJ_TILE = 512
PROMPT_MERGE_TILE = 512
PROMPT_ATTN_BLOCK = 256
DECODE_PAGES_PER_STEP = 16


def kernel(x_prompt, x_sample, c_prompt, c_sample, cache_sba_k, cache_sba_v, cache_diff_k, cache_diff_v, page_table, norm_w, w_ada, b_ada, ffn_w_in, ffn_w_out, w_in, w_o_sba, w_o_diff, w_out, diff_lambda, diff_subln_w, final_norm_w):
    batch, seq, d = x_prompt.shape
    n_seq, n_new, _ = x_sample.shape
    depth = norm_w.shape[0]
    n_pool = cache_sba_k.shape[1]
    past = page_table.shape[1] * PAGE_SIZE
    sb = N_SB_HEADS * HEAD_DIM
    df = N_DF_HEADS * 2 * HEAD_DIM
    t_s = n_seq * n_new

    mod = _ada_mod(jnp.concatenate([c_prompt, c_sample], axis=0), w_ada, b_ada)
    mod = mod.reshape(depth, batch + n_seq, N_SUBLAYERS, 3, d)

    sb_kt = jnp.transpose(cache_sba_k, (0, 1, 3, 4, 2)).reshape(depth, n_pool, sb, PAGE_SIZE)
    sb_vt = jnp.transpose(cache_sba_v, (0, 1, 3, 4, 2)).reshape(depth, n_pool, sb, PAGE_SIZE)
    df_k = cache_diff_k.reshape(depth, n_pool, PAGE_SIZE * N_DF_HEADS, 2 * HEAD_DIM)
    df_v = cache_diff_v.reshape(depth, n_pool, PAGE_SIZE * N_DF_HEADS, 2 * HEAD_DIM)

    cos_p, sin_p = _rope_tables(jnp.arange(seq, dtype=jnp.int32), N_SB_HEADS)
    cos_s, sin_s = _rope_tables(jnp.tile(past + jnp.arange(n_new, dtype=jnp.int32), n_seq), N_SB_HEADS)

    xp = x_prompt.reshape(batch * seq, d)
    xs = x_sample.reshape(t_s, d)
    kv_prompt = None
    kv_sample = []
    for layer in range(depth):
        last = layer == depth - 1
        wi_a, wo_a = _prep_ffn_weights(ffn_w_in[layer, 0], ffn_w_out[layer, 0], FFN_CHUNK)
        wi_b, wo_b = _prep_ffn_weights(ffn_w_in[layer, 1], ffn_w_out[layer, 1], FFN_CHUNK)
        w_b = w_in[layer].astype(BF16)
        w_t = w_b[:, :3 * sb + df].T
        w_n = jnp.concatenate([w_b[:, sb:2 * sb], w_b[:, 3 * sb + df:]], axis=1)
        wos, wod, wout = w_o_sba[layer].astype(BF16), w_o_diff[layer].astype(BF16), w_out[layer].astype(BF16)
        mp = mod[layer, :batch]
        ms = jnp.repeat(mod[layer, batch:], n_new, axis=0)

        pm = lambda sub, kind: mp[:, sub, kind][:, None, :]
        xp = _ffn(xp, norm_w[layer, 0], pm(0, 0), pm(0, 1), pm(0, 2), wi_a, wo_a, final_norm_w, tm=min(PROMPT_FFN_TILE, seq), final_norm=False)
        qt_sb, qt_df, k_tok, kt, vt, kdf, vdf, g = _in_proj_prompt(
            xp, norm_w[layer, 1], pm(1, 0), pm(1, 1), cos_p, sin_p, cos_p.T, sin_p.T, w_t, w_n, kv_prompt, layer,
            depth=depth, batch=batch, seq=seq, tm=min(PROMPT_PROJ_TILE, seq))
        kv_prompt = (kt, vt, kdf, vdf)
        o_sb = _prompt_sb_t(qt_sb, k_tok, vt, layer, batch=batch, seq=seq, blk=PROMPT_ATTN_BLOCK)
        o_df = _prompt_diff_t(qt_df, kdf, vdf, diff_lambda, diff_subln_w, layer, batch=batch, seq=seq, blk=PROMPT_ATTN_BLOCK)
        xp = _merge(xp, o_sb, o_df, g, pm(1, 2), wos, wod, wout, tm=min(PROMPT_MERGE_TILE, seq))
        xp = _ffn(xp, norm_w[layer, 2], pm(2, 0), pm(2, 1), pm(2, 2), wi_b, wo_b, final_norm_w, tm=min(PROMPT_FFN_TILE, seq), final_norm=last)

        sm = lambda sub, kind: ms[:, sub, kind][None]
        xs = _ffn(xs, norm_w[layer, 0], sm(0, 0), sm(0, 1), sm(0, 2), wi_a, wo_a, final_norm_w, tm=t_s, final_norm=False)
        q_sb, k_sb, v_sb, q_df, k_df, v_df, g = _in_proj_sample(xs, norm_w[layer, 1], ms[:, 1, 0], ms[:, 1, 1], cos_s, sin_s, w_b)
        kv_sample.append((k_sb, v_sb, k_df, v_df))
        o_sb = _decode_sb(page_table, q_sb, k_sb, v_sb, sb_kt, sb_vt, layer, n_new=n_new)
        o_df = _decode_diff(
            page_table, q_df, k_df.reshape(n_seq, N_DF_HEADS * n_new, 2 * HEAD_DIM), v_df.reshape(n_seq, N_DF_HEADS * n_new, 2 * HEAD_DIM),
            df_k, df_v, diff_lambda, diff_subln_w, layer, n_new=n_new, pages_per_step=min(DECODE_PAGES_PER_STEP, page_table.shape[1]))
        xs = _merge(xs, o_sb, o_df, g, sm(1, 2), wos, wod, wout, tm=t_s)
        xs = _ffn(xs, norm_w[layer, 2], sm(2, 0), sm(2, 1), sm(2, 2), wi_b, wo_b, final_norm_w, tm=t_s, final_norm=last)

    kt, vt, kdf, vdf = kv_prompt
    stack = lambda n, shape: jnp.stack([kv[n] for kv in kv_sample]).reshape(shape)
    return (
        xp.reshape(batch, seq, d),
        xs.reshape(n_seq, n_new, d),
        jnp.transpose(kt.reshape(depth, batch, N_SB_HEADS, HEAD_DIM, seq), (0, 1, 4, 2, 3)),
        jnp.transpose(vt.reshape(depth, batch, N_SB_HEADS, HEAD_DIM, seq), (0, 1, 4, 2, 3)),
        kdf.reshape(depth, batch, seq, N_DF_HEADS, 2 * HEAD_DIM),
        vdf.reshape(depth, batch, seq, N_DF_HEADS, 2 * HEAD_DIM),
        stack(0, (depth, n_seq, n_new, N_SB_HEADS, HEAD_DIM)),
        stack(1, (depth, n_seq, n_new, N_SB_HEADS, HEAD_DIM)),
        stack(2, (depth, n_seq, n_new, N_DF_HEADS, 2 * HEAD_DIM)),
        stack(3, (depth, n_seq, n_new, N_DF_HEADS, 2 * HEAD_DIM)),
    )
```

```python
import functools
import math

import jax
import jax.numpy as jnp
from jax import lax
from jax.experimental import pallas as pl
from jax.experimental.pallas import tpu as pltpu

HEAD_DIM = 64
N_SB_HEADS = 8
N_DF_HEADS = 4
PAGE_SIZE = 128
ROPE_THETA = 10000.0
NORM_EPS = 1e-6
SUBLN_EPS = 1e-5
N_SUBLAYERS = 3
SB_LOG_WEIGHT_FLOOR = -104.0
V7X_VMEM_LIMIT_BYTES = 56 * 1024 * 1024
MASK_VALUE = -1e30

F32 = jnp.float32
BF16 = jnp.bfloat16


def _cparams(n_axes):
    return pltpu.CompilerParams(
        dimension_semantics=("arbitrary",) * n_axes,
        vmem_limit_bytes=V7X_VMEM_LIMIT_BYTES,
    )


def _resident(shape):
    nd = len(shape)
    return pl.BlockSpec(shape, lambda *_: (0,) * nd, pipeline_mode=pl.Buffered(1))


def _dot(a, b):
    return jnp.dot(a, b, preferred_element_type=F32)


def _dot_nt(a, b):
    return lax.dot_general(a, b, (((1,), (1,)), ((), ())), preferred_element_type=F32)


def _dot_tn(a, b):
    return lax.dot_general(a, b, (((0,), (0,)), ((), ())), preferred_element_type=F32)


def _split_bf16(x):
    hi = x.astype(BF16)
    lo = (x - hi.astype(F32)).astype(BF16)
    return hi, lo


def _rms(x, eps):
    return x * lax.rsqrt(jnp.mean(x * x, axis=-1, keepdims=True) + eps)


def _sigmoid(x):
    return 1.0 / (1.0 + jnp.exp(-x))


def _ada_mod_kernel(c_ref, w_ref, b_ref, o_ref):
    c = c_ref[...]
    a_hi, a_lo = _split_bf16(c * _sigmoid(c))
    w_hi, w_lo = _split_bf16(w_ref[...])
    o_ref[...] = _dot(a_hi, w_hi) + (_dot(a_hi, w_lo) + _dot(a_lo, w_hi)) + b_ref[...]


def _ada_mod(c_all, w_ada, b_ada):
    n, d = c_all.shape
    depth, _, width = w_ada.shape
    tn = 1536
    assert width % tn == 0
    return pl.pallas_call(
        _ada_mod_kernel,
        grid=(depth, width // tn),
        in_specs=[
            pl.BlockSpec((n, d), lambda l, j: (0, 0)),
            pl.BlockSpec((None, d, tn), lambda l, j: (l, 0, j)),
            pl.BlockSpec((None, 1, tn), lambda l, j: (l, 0, j)),
        ],
        out_specs=pl.BlockSpec((None, n, tn), lambda l, j: (l, 0, j)),
        out_shape=jax.ShapeDtypeStruct((depth, n, width), F32),
        compiler_params=_cparams(2),
        name="ada_mod",
    )(c_all, w_ada, b_ada.reshape(depth, 1, width))


def _ffn_kernel(x_ref, nw_ref, shift_ref, scale_ref, gate_ref, wi_ref, wo_ref, fw_ref, o_ref, *, n_chunks, final_norm):
    x = x_ref[...]
    h = _rms(x, NORM_EPS) * nw_ref[...]
    hb = (h * (1.0 + scale_ref[...]) + shift_ref[...]).astype(BF16)
    f = wo_ref.shape[0]
    fc = f // n_chunks
    acc = None
    for j in range(n_chunks):
        g = _dot(hb, wi_ref[:, j * fc:(j + 1) * fc])
        u = _dot(hb, wi_ref[:, f + j * fc:f + (j + 1) * fc])
        a = (g * _sigmoid(g) * u).astype(BF16)
        part = _dot(a, wo_ref[j * fc:(j + 1) * fc, :])
        acc = part if acc is None else acc + part
    y = x + (0.5 * gate_ref[...]) * acc
    if final_norm:
        y = _rms(y, NORM_EPS) * fw_ref[...]
    o_ref[...] = y


def _ffn(x, norm_w, shift, scale, gate, wi, wo, final_w, *, tm, final_norm):
    t, d = x.shape
    groups, r, _ = shift.shape
    assert wo.shape[0] % FFN_CHUNK == 0
    n_chunks = wo.shape[0] // FFN_CHUNK
    assert t % (tm * groups) == 0
    tiles_per_group = t // (tm * groups)
    mod_spec = pl.BlockSpec((None, r, d), lambda i: (i // tiles_per_group, 0, 0))
    row_spec = pl.BlockSpec((1, d), lambda i: (0, 0))
    return pl.pallas_call(
        functools.partial(_ffn_kernel, n_chunks=n_chunks, final_norm=final_norm),
        grid=(t // tm,),
        in_specs=[
            pl.BlockSpec((tm, d), lambda i: (i, 0)),
            row_spec, mod_spec, mod_spec, mod_spec,
            _resident(wi.shape), _resident(wo.shape),
            row_spec,
        ],
        out_specs=pl.BlockSpec((tm, d), lambda i: (i, 0)),
        out_shape=jax.ShapeDtypeStruct((t, d), F32),
        compiler_params=_cparams(1),
        name="ffn",
    )(x, norm_w.reshape(1, d), shift, scale, gate, wi, wo, final_w.reshape(1, d))


def _rope(y, cos, sin_signed):
    width = y.shape[-1]
    half = HEAD_DIM // 2
    lane = lax.broadcasted_iota(jnp.int32, y.shape, 1)
    partner = jnp.where(lane % HEAD_DIM < half, pltpu.roll(y, width - half, 1), pltpu.roll(y, half, 1))
    return y * cos + partner * sin_signed


def _rope_t(yt, cos_t, sin_signed_t):
    n_rows = yt.shape[0]
    half = HEAD_DIM // 2
    r = lax.broadcasted_iota(jnp.int32, yt.shape, 0)
    partner = jnp.where(r % HEAD_DIM < half, pltpu.roll(yt, n_rows - half, 0), pltpu.roll(yt, half, 0))
    return yt * cos_t + partner * sin_signed_t


def _adaln_bf16(x_ref, nw_ref, shift_ref, scale_ref):
    h = _rms(x_ref[...], NORM_EPS) * nw_ref[...]
    return (h * (1.0 + scale_ref[...]) + shift_ref[...]).astype(BF16)


def _in_proj_prompt_kernel(x_ref, nw_ref, shift_ref, scale_ref, cos_ref, sin_ref, cos_t_ref, sin_t_ref, wt_ref, wn_ref,
                           qsb_t_ref, qdf_t_ref, ktok_ref, ksb_t_ref, vsb_t_ref, kdf_ref, vdf_ref, g_ref):
    sb = N_SB_HEADS * HEAD_DIM
    df = N_DF_HEADS * 2 * HEAD_DIM
    tm = x_ref.shape[0]
    scale = HEAD_DIM ** -0.5
    hb = _adaln_bf16(x_ref, nw_ref, shift_ref, scale_ref)
    yt = _dot_nt(wt_ref[...], hb)
    qsb_t_ref[...] = (yt[0:sb] * scale).astype(BF16)
    ksb_t_ref[...] = yt[sb:2 * sb]
    vsb_t_ref[...] = yt[2 * sb:3 * sb]
    qdf_t_ref[...] = (_rope_t(yt[3 * sb:3 * sb + df], cos_t_ref[...], sin_t_ref[...]) * scale).astype(BF16)
    yn = _dot(hb, wn_ref[...])
    ktok_ref[...] = yn[:, 0:sb].astype(BF16)
    k_df = _rope(yn[:, sb:sb + df], cos_ref[...], sin_ref[...])
    v_df = yn[:, sb + df:sb + 2 * df]
    g_ref[...] = yn[:, sb + 2 * df:]
    width = 2 * HEAD_DIM
    for head in range(N_DF_HEADS):
        kdf_ref[pl.ds(head, tm, stride=N_DF_HEADS), :] = k_df[:, head * width:(head + 1) * width]
        vdf_ref[pl.ds(head, tm, stride=N_DF_HEADS), :] = v_df[:, head * width:(head + 1) * width]


def _in_proj_sample_kernel(x_ref, nw_ref, shift_ref, scale_ref, cos_ref, sin_ref, w_ref,
                           qsb_ref, ksb_ref, vsb_ref, qdf_ref, kdf_ref, vdf_ref, g_ref):
    sb = N_SB_HEADS * HEAD_DIM
    df = N_DF_HEADS * 2 * HEAD_DIM
    hb = _adaln_bf16(x_ref, nw_ref, shift_ref, scale_ref)
    cos = cos_ref[...]
    sin = sin_ref[...]
    qsb_ref[...] = _dot(hb, w_ref[:, 0:sb])
    ksb_ref[...] = _dot(hb, w_ref[:, sb:2 * sb])
    vsb_ref[...] = _dot(hb, w_ref[:, 2 * sb:3 * sb])
    qdf_ref[...] = _rope(_dot(hb, w_ref[:, 3 * sb:3 * sb + df]), cos, sin)
    kdf_ref[...] = _rope(_dot(hb, w_ref[:, 3 * sb + df:3 * sb + 2 * df]), cos, sin)
    vdf_ref[...] = _dot(hb, w_ref[:, 3 * sb + 2 * df:3 * sb + 3 * df])
    g_ref[...] = _dot(hb, w_ref[:, 3 * sb + 3 * df:])


def _in_proj_prompt(x, norm_w, shift, scale, cos, sin, cos_t, sin_t, w_t, w_n, kv_prev, layer, *, depth, batch, seq, tm):
    t, d = x.shape
    sb = N_SB_HEADS * HEAD_DIM
    df = N_DF_HEADS * 2 * HEAD_DIM
    tiles_per_seq = seq // tm
    tok = lambda i: (i, 0)
    const = lambda i: (0, 0)
    mod_spec = pl.BlockSpec((None, 1, d), lambda i: (i // tiles_per_seq, 0, 0))
    tab_spec = pl.BlockSpec((tm, df), lambda i: (i % tiles_per_seq, 0))
    tab_t_spec = pl.BlockSpec((df, tm), lambda i: (0, i % tiles_per_seq))
    in_specs = [
        pl.BlockSpec((tm, d), tok), pl.BlockSpec((1, d), const), mod_spec, mod_spec, tab_spec, tab_spec, tab_t_spec, tab_t_spec,
        _resident(w_t.shape), _resident(w_n.shape),
    ]
    args = [x, norm_w.reshape(1, d), shift, scale, cos, sin, cos_t, sin_t, w_t, w_n]
    n_in = len(args)
    aliases = {}
    if kv_prev is not None:
        for n, buf in enumerate(kv_prev):
            in_specs.append(pl.BlockSpec(memory_space=pl.ANY))
            aliases[len(args)] = (3, 4, 5, 6)[n]
            args.append(buf)
    n_alias = len(aliases)
    qt_spec = pl.BlockSpec((None, sb, tm), lambda i: (i // tiles_per_seq, 0, i % tiles_per_seq))
    kt_spec = pl.BlockSpec((None, None, sb, tm), lambda i: (layer, i // tiles_per_seq, 0, i % tiles_per_seq))
    il_spec = pl.BlockSpec((None, None, N_DF_HEADS * tm, 2 * HEAD_DIM), lambda i: (layer, i // tiles_per_seq, i % tiles_per_seq, 0))
    qt_shape = jax.ShapeDtypeStruct((batch, sb, seq), BF16)
    kt_shape = jax.ShapeDtypeStruct((depth, batch, sb, seq), F32)
    il_shape = jax.ShapeDtypeStruct((depth, batch, N_DF_HEADS * seq, 2 * HEAD_DIM), F32)

    def body(*refs):
        _in_proj_prompt_kernel(*refs[:n_in], *refs[n_in + n_alias:])

    return pl.pallas_call(
        body,
        grid=(t // tm,),
        in_specs=in_specs,
        out_specs=[qt_spec, qt_spec, pl.BlockSpec((tm, sb), tok), kt_spec, kt_spec, il_spec, il_spec, pl.BlockSpec((tm, 2 * d), tok)],
        out_shape=[qt_shape, qt_shape, jax.ShapeDtypeStruct((t, sb), BF16), kt_shape, kt_shape, il_shape, il_shape,
                   jax.ShapeDtypeStruct((t, 2 * d), F32)],
        input_output_aliases=aliases,
        compiler_params=_cparams(1),
        name="in_proj_prompt",
    )(*args)


def _in_proj_sample(x, norm_w, shift, scale, cos, sin, w_b):
    t, d = x.shape
    sb = N_SB_HEADS * HEAD_DIM
    df = N_DF_HEADS * 2 * HEAD_DIM
    full = lambda shape: pl.BlockSpec(shape, lambda i: (0,) * len(shape))
    widths = (sb, sb, sb, df, df, df, 2 * d)
    return pl.pallas_call(
        _in_proj_sample_kernel,
        grid=(1,),
        in_specs=[full((t, d)), full((1, d)), full((t, d)), full((t, d)), full((t, df)), full((t, df)), full(w_b.shape)],
        out_specs=[full((t, w)) for w in widths],
        out_shape=[jax.ShapeDtypeStruct((t, w), F32) for w in widths],
        compiler_params=_cparams(1),
        name="in_proj_sample",
    )(x, norm_w.reshape(1, d), shift, scale, cos, sin, w_b)


def _rope_tables(pos, n_heads):
    half = HEAD_DIM // 2
    inv = 1.0 / (ROPE_THETA ** (jnp.arange(half, dtype=F32) / half))
    ang = pos.astype(F32)[:, None] * inv[None, :]
    cos, sin = jnp.cos(ang), jnp.sin(ang)
    cos = jnp.tile(jnp.concatenate([cos, cos], axis=1), (1, n_heads))
    sin = jnp.tile(jnp.concatenate([-sin, sin], axis=1), (1, n_heads))
    return cos, sin


def _later_column_sum_matrix(n):
    row = lax.broadcasted_iota(jnp.int32, (2 * n, n), 0) % n
    col = lax.broadcasted_iota(jnp.int32, (2 * n, n), 1)
    return jnp.where(row > col, 1.0, 0.0).astype(BF16)


def _sb_block(z, allowed, carry, tri2):
    log_beta = jnp.minimum(z, 0.0) - jnp.log(1.0 + jnp.exp(-jnp.abs(z)))
    log_keep = log_beta - z
    if allowed is not None:
        log_keep = jnp.where(allowed, log_keep, 0.0)
    hi, lo = _split_bf16(log_keep)
    log_between = _dot(jnp.concatenate([hi, lo], axis=1), tri2)
    w = jnp.exp(log_beta + log_between + carry)
    if allowed is not None:
        w = jnp.where(allowed, w, 0.0)
    return w, carry + jnp.sum(log_keep, axis=-1, keepdims=True)


def _diff_lambda(dl_ref, lam_init):
    lp = dl_ref[...]
    a = jnp.sum(lp[0:1] * lp[1:2], axis=-1, keepdims=True)
    b = jnp.sum(lp[2:3] * lp[3:4], axis=-1, keepdims=True)
    return jnp.exp(a) - jnp.exp(b) + lam_init


def _sub_norm(o, subw, lam_init):
    return _rms(o, SUBLN_EPS) * subw * (1.0 - lam_init)


def _lam_init(layer):
    return 0.8 - 0.6 * math.exp(-0.3 * layer)


def _later_row_sum_matrix(n):
    row = lax.broadcasted_iota(jnp.int32, (n, 2 * n), 0)
    col = lax.broadcasted_iota(jnp.int32, (n, 2 * n), 1) % n
    return jnp.where(col > row, 1.0, 0.0).astype(BF16)


def _sb_block_t(zt, allowed, carry, u2):
    log_beta = jnp.minimum(zt, 0.0) - jnp.log(1.0 + jnp.exp(-jnp.abs(zt)))
    log_keep = log_beta - zt
    if allowed is not None:
        log_keep = jnp.where(allowed, log_keep, 0.0)
    hi, lo = _split_bf16(log_keep)
    log_between = _dot(u2, jnp.concatenate([hi, lo], axis=0))
    w = jnp.exp(log_beta + log_between + carry)
    if allowed is not None:
        w = jnp.where(allowed, w, 0.0)
    return w, carry + jnp.sum(log_keep, axis=0, keepdims=True)


def _split_cols(qt, upper):
    q = qt.astype(F32)
    return jnp.concatenate([jnp.where(upper, q, 0.0), jnp.where(upper, 0.0, q)], axis=1).astype(BF16)


def _prompt_sb_t_kernel(qt_ref, k_ref, vt_ref, o_ref, *, blk):
    seq = k_ref.shape[0]
    n_blk = seq // blk
    width = 2 * HEAD_DIM
    pairs = N_SB_HEADS // 2
    cols = pairs * 2 * blk
    u2 = _later_row_sum_matrix(blk)
    upper = lax.broadcasted_iota(jnp.int32, (width, blk), 0) < HEAD_DIM

    def q_block(iq, carry):
        q0 = pl.multiple_of(iq * blk, blk)
        qs = [_split_cols(qt_ref[p * width:(p + 1) * width, pl.ds(q0, blk)], upper) for p in range(pairs)]

        def step(jk, c, accs, masked):
            k0 = pl.multiple_of(jk * blk, blk)
            zt = jnp.concatenate([_dot(k_ref[pl.ds(k0, blk), p * width:(p + 1) * width], qs[p]) for p in range(pairs)], axis=1)
            allowed = None
            if masked:
                key = lax.broadcasted_iota(jnp.int32, (blk, cols), 0)
                query = lax.broadcasted_iota(jnp.int32, (blk, cols), 1) % blk
                allowed = key < query
            w, c = _sb_block_t(zt, allowed, c, u2)
            wb = w.astype(BF16)
            accs = tuple(
                accs[p] + _dot(vt_ref[p * width:(p + 1) * width, pl.ds(k0, blk)].astype(BF16), wb[:, p * 2 * blk:(p + 1) * 2 * blk])
                for p in range(pairs))
            return c, accs

        zero_acc = tuple(jnp.zeros((width, 2 * blk), F32) for _ in range(pairs))
        c, accs = step(iq, jnp.zeros((1, cols), F32), zero_acc, True)

        def cond(st):
            jk, c, _ = st
            return jnp.logical_and(jk >= 0, jnp.max(c) >= SB_LOG_WEIGHT_FLOOR)

        def body(st):
            jk, c, accs = st
            c, accs = step(jk, c, accs, False)
            return jk - 1, c, accs

        _, _, accs = lax.while_loop(cond, body, (iq - 1, c, accs))
        for p in range(pairs):
            o_ref[p * width:(p + 1) * width, pl.ds(q0, blk)] = jnp.where(upper, accs[p][:, :blk], accs[p][:, blk:])
        return carry

    lax.fori_loop(0, n_blk, q_block, 0)


def _prompt_sb_t(qt, k_tok, vt, layer, *, batch, seq, blk):
    sb = qt.shape[1]
    t_spec = pl.BlockSpec((None, sb, seq), lambda b: (b, 0, 0))
    return pl.pallas_call(
        functools.partial(_prompt_sb_t_kernel, blk=blk),
        grid=(batch,),
        in_specs=[t_spec, pl.BlockSpec((seq, sb), lambda b: (b, 0)), pl.BlockSpec((None, None, sb, seq), lambda b: (layer, b, 0, 0))],
        out_specs=t_spec,
        out_shape=jax.ShapeDtypeStruct((batch, sb, seq), F32),
        compiler_params=_cparams(1),
        name="prompt_sb",
    )(qt, k_tok, vt)


def _prompt_diff_t_kernel(qt_ref, k_ref, v_ref, dl_ref, subw_ref, o_ref, *, blk, lam_init):
    seq = qt_ref.shape[1]
    n_blk = seq // blk
    width = 2 * HEAD_DIM
    cols = N_DF_HEADS * 2 * blk
    lam = _diff_lambda(dl_ref, lam_init)
    subw = subw_ref[...]
    upper = lax.broadcasted_iota(jnp.int32, (width, blk), 0) < HEAD_DIM

    def q_block(iq, carry):
        q0 = pl.multiple_of(iq * blk, blk)
        qs = [_split_cols(qt_ref[h * width:(h + 1) * width, pl.ds(q0, blk)], upper) for h in range(N_DF_HEADS)]

        def step(jk, state, masked):
            m, l, accs = state
            base = jk * (blk * N_DF_HEADS)
            ks = [k_ref[pl.ds(base + h, blk, stride=N_DF_HEADS), :].astype(BF16) for h in range(N_DF_HEADS)]
            vs = [v_ref[pl.ds(base + h, blk, stride=N_DF_HEADS), :].astype(BF16) for h in range(N_DF_HEADS)]
            st = jnp.concatenate([_dot(ks[h], qs[h]) for h in range(N_DF_HEADS)], axis=1)
            if masked:
                key = lax.broadcasted_iota(jnp.int32, (blk, cols), 0)
                query = lax.broadcasted_iota(jnp.int32, (blk, cols), 1) % blk
                st = jnp.where(key <= query, st, MASK_VALUE)
            m_new = jnp.maximum(m, jnp.max(st, axis=0, keepdims=True))
            alpha = jnp.exp(m - m_new)
            p = jnp.exp(st - m_new)
            l = alpha * l + jnp.sum(p, axis=0, keepdims=True)
            pb = p.astype(BF16)
            accs = tuple(
                alpha[:, h * 2 * blk:(h + 1) * 2 * blk] * accs[h] + _dot_tn(vs[h], pb[:, h * 2 * blk:(h + 1) * 2 * blk])
                for h in range(N_DF_HEADS))
            return m_new, l, accs

        state = (jnp.full((1, cols), MASK_VALUE, F32), jnp.zeros((1, cols), F32),
                 tuple(jnp.zeros((width, 2 * blk), F32) for _ in range(N_DF_HEADS)))
        state = lax.fori_loop(0, iq, lambda jk, st: step(jk, st, False), state)
        _, l, accs = step(iq, state, True)
        for h in range(N_DF_HEADS):
            a = accs[h] / l[:, h * 2 * blk:(h + 1) * 2 * blk]
            o = a[:, :blk] - lam * a[:, blk:]
            o = o * lax.rsqrt(jnp.mean(o * o, axis=0, keepdims=True) + SUBLN_EPS)
            o_ref[h * width:(h + 1) * width, pl.ds(q0, blk)] = o * subw * (1.0 - lam_init)
        return carry

    lax.fori_loop(0, n_blk, q_block, 0)


def _prompt_diff_t(qt, kdf, vdf, diff_lambda, subln_w, layer, *, batch, seq, blk):
    df = qt.shape[1]
    width = 2 * HEAD_DIM
    t_spec = pl.BlockSpec((None, df, seq), lambda b: (b, 0, 0))
    il_spec = pl.BlockSpec((None, None, N_DF_HEADS * seq, width), lambda b: (layer, b, 0, 0))
    return pl.pallas_call(
        functools.partial(_prompt_diff_t_kernel, blk=blk, lam_init=_lam_init(layer)),
        grid=(batch,),
        in_specs=[
            t_spec, il_spec, il_spec,
            pl.BlockSpec((None, 4, HEAD_DIM), lambda b: (layer, 0, 0)),
            pl.BlockSpec((None, width, 1), lambda b: (layer, 0, 0)),
        ],
        out_specs=t_spec,
        out_shape=jax.ShapeDtypeStruct((batch, df, seq), F32),
        compiler_params=_cparams(1),
        name="prompt_diff",
    )(qt, kdf, vdf, diff_lambda, subln_w.reshape(subln_w.shape[0], width, 1))


def _decode_sb_kernel(pt_ref, q_ref, kn_ref, vn_ref, kc_ref, vc_ref, o_ref, kbuf, vbuf, sem, *, layer, n_pages):
    b = pl.program_id(0)
    n_new, sb = q_ref.shape
    rows = N_SB_HEADS * n_new

    def page_copies(j, slot):
        page = pt_ref[b, j]
        return (pltpu.make_async_copy(kc_ref.at[layer, page], kbuf.at[slot], sem.at[0, slot]),
                pltpu.make_async_copy(vc_ref.at[layer, page], vbuf.at[slot], sem.at[1, slot]))

    def start_page(j, slot):
        for cp in page_copies(j, slot):
            cp.start()

    def wait_page(j, slot):
        for cp in page_copies(j, slot):
            cp.wait()

    start_page(n_pages - 1, (n_pages - 1) % 2)

    tri = _later_column_sum_matrix(PAGE_SIZE)
    row_head = lax.broadcasted_iota(jnp.int32, (rows, sb), 0) // n_new
    lane_head = lax.broadcasted_iota(jnp.int32, (rows, sb), 1) // HEAD_DIM
    own_head = row_head == lane_head
    q = jnp.tile(q_ref[...] * (HEAD_DIM ** -0.5), (N_SB_HEADS, 1))
    qbd = jnp.where(own_head, q, 0.0).astype(BF16)

    pad = jnp.zeros((PAGE_SIZE - n_new, sb), F32)
    k_new = jnp.concatenate([kn_ref[...], pad], axis=0).astype(BF16)
    v_new = jnp.concatenate([vn_ref[...], pad], axis=0).astype(BF16)
    query = lax.broadcasted_iota(jnp.int32, (rows, PAGE_SIZE), 0) % n_new
    col = lax.broadcasted_iota(jnp.int32, (rows, PAGE_SIZE), 1)
    w, c = _sb_block(_dot_nt(qbd, k_new), col < query, jnp.zeros((rows, 1), F32), tri)
    acc = _dot(w.astype(BF16), v_new)

    def cond(st):
        j, c, _ = st
        return jnp.logical_and(j >= 0, jnp.max(c) >= SB_LOG_WEIGHT_FLOOR)

    def body(st):
        j, c, acc = st
        slot = j % 2
        wait_page(j, slot)

        @pl.when(j > 0)
        def _():
            start_page(j - 1, 1 - slot)

        w, c = _sb_block(_dot(qbd, kbuf[slot].astype(BF16)), None, c, tri)
        return j - 1, c, acc + _dot_nt(w.astype(BF16), vbuf[slot].astype(BF16))

    j_end, _, acc = lax.while_loop(cond, body, (n_pages - 1, c, acc))

    @pl.when(j_end >= 0)
    def _():
        wait_page(j_end, j_end % 2)

    acc = jnp.where(own_head, acc, 0.0)
    out = acc[0:n_new]
    for head in range(1, N_SB_HEADS):
        out = out + acc[head * n_new:(head + 1) * n_new]
    o_ref[...] = out


def _decode_sb(page_table, q, k_new, v_new, cache_kt, cache_vt, layer, *, n_new):
    t, sb = q.shape
    n_seq, n_pages = page_table.shape
    tok = pl.BlockSpec((n_new, sb), lambda b, pt: (b, 0))
    any_spec = pl.BlockSpec(memory_space=pl.ANY)
    return pl.pallas_call(
        functools.partial(_decode_sb_kernel, layer=layer, n_pages=n_pages),
        grid_spec=pltpu.PrefetchScalarGridSpec(
            num_scalar_prefetch=1,
            grid=(n_seq,),
            in_specs=[tok, tok, tok, any_spec, any_spec],
            out_specs=tok,
            scratch_shapes=[
                pltpu.VMEM((2, sb, PAGE_SIZE), F32),
                pltpu.VMEM((2, sb, PAGE_SIZE), F32),
                pltpu.SemaphoreType.DMA((2, 2)),
            ],
        ),
        out_shape=jax.ShapeDtypeStruct((t, sb), F32),
        compiler_params=_cparams(1),
        name="decode_sb",
    )(page_table, q, k_new, v_new, cache_kt, cache_vt)


def _decode_diff_kernel(*refs, layer, pages_per_step, lam_init):
    (pt_ref, q_ref, kn_ref, vn_ref, kc_ref, vc_ref, dl_ref, subw_ref, o_ref,
     kbuf, vbuf, sem, m_ref, l_ref, acc_ref, qall_ref) = refs
    seq_id = pl.program_id(0)
    step = pl.program_id(1)
    n_steps = pl.num_programs(1)
    n_new = q_ref.shape[0]
    width = 2 * HEAD_DIM
    rows = N_DF_HEADS * 2 * n_new

    linear = seq_id * n_steps + step
    slot = linear % 2

    def step_copies(s_id, st, sl):
        copies = []
        for r in range(pages_per_step):
            page = pt_ref[s_id, st * pages_per_step + r]
            copies.append(pltpu.make_async_copy(kc_ref.at[layer, page], kbuf.at[sl, r], sem.at[0, sl]))
            copies.append(pltpu.make_async_copy(vc_ref.at[layer, page], vbuf.at[sl, r], sem.at[1, sl]))
        return copies

    @pl.when(linear == 0)
    def _():
        for cp in step_copies(seq_id, step, slot):
            cp.start()

    @pl.when(linear + 1 < pl.num_programs(0) * n_steps)
    def _():
        nxt = linear + 1
        for cp in step_copies(nxt // n_steps, nxt % n_steps, 1 - slot):
            cp.start()

    for cp in step_copies(seq_id, step, slot):
        cp.wait()

    @pl.when(step == 0)
    def _():
        m_ref[...] = jnp.full(m_ref.shape, MASK_VALUE, F32)
        l_ref[...] = jnp.zeros(l_ref.shape, F32)
        acc_ref[...] = jnp.zeros(acc_ref.shape, F32)
        q = q_ref[...] * (HEAD_DIM ** -0.5)
        first_half = lax.broadcasted_iota(jnp.int32, (n_new, width), 1) < HEAD_DIM
        pieces = []
        for head in range(N_DF_HEADS):
            qh = q[:, head * width:(head + 1) * width]
            pieces += [jnp.where(first_half, qh, 0.0), jnp.where(first_half, 0.0, qh)]
        qall_ref[...] = jnp.concatenate(pieces, axis=0).astype(BF16)

    per_head = 2 * n_new
    qall = qall_ref[...]

    def scores(k_heads):
        return jnp.concatenate(
            [_dot_nt(qall[h * per_head:(h + 1) * per_head], k_heads[h]) for h in range(N_DF_HEADS)], axis=0)

    def update(s, v_heads):
        m = m_ref[...]
        m_new = jnp.maximum(m, jnp.max(s, axis=-1, keepdims=True))
        alpha = jnp.exp(m - m_new)
        p = jnp.exp(s - m_new)
        l_ref[...] = alpha * l_ref[...] + jnp.sum(p, axis=-1, keepdims=True)
        pb = p.astype(BF16)
        pv = jnp.concatenate(
            [_dot(pb[h * per_head:(h + 1) * per_head], v_heads[h]) for h in range(N_DF_HEADS)], axis=0)
        acc_ref[...] = alpha * acc_ref[...] + pv
        m_ref[...] = m_new

    def head_rows(buf, h):
        return jnp.concatenate(
            [buf[slot, r, pl.ds(h, PAGE_SIZE, stride=N_DF_HEADS), :] for r in range(pages_per_step)], axis=0).astype(BF16)

    update(scores([head_rows(kbuf, h) for h in range(N_DF_HEADS)]), [head_rows(vbuf, h) for h in range(N_DF_HEADS)])

    @pl.when(step == pl.num_programs(1) - 1)
    def _():
        pad = jnp.zeros((PAGE_SIZE - n_new, width), F32)

        def new_rows(ref, h):
            return jnp.concatenate([ref[pl.ds(h, n_new, stride=N_DF_HEADS), :], pad], axis=0).astype(BF16)

        query = lax.broadcasted_iota(jnp.int32, (rows, PAGE_SIZE), 0) % n_new
        col = lax.broadcasted_iota(jnp.int32, (rows, PAGE_SIZE), 1)
        s_new = jnp.where(col <= query, scores([new_rows(kn_ref, h) for h in range(N_DF_HEADS)]), MASK_VALUE)
        update(s_new, [new_rows(vn_ref, h) for h in range(N_DF_HEADS)])
        o = acc_ref[...] / l_ref[...]
        lam = _diff_lambda(dl_ref, lam_init)
        subw = subw_ref[...]
        for head in range(N_DF_HEADS):
            o1 = o[(2 * head) * n_new:(2 * head + 1) * n_new]
            o2 = o[(2 * head + 1) * n_new:(2 * head + 2) * n_new]
            o_ref[:, head * width:(head + 1) * width] = _sub_norm(o1 - lam * o2, subw, lam_init)


def _decode_diff(page_table, q, k_new, v_new, cache_k, cache_v, diff_lambda, subln_w, layer, *, n_new, pages_per_step):
    t, df = q.shape
    n_seq, n_pages = page_table.shape
    assert n_pages % pages_per_step == 0
    width = 2 * HEAD_DIM
    rows = N_DF_HEADS * 2 * n_new
    page_rows = PAGE_SIZE * N_DF_HEADS
    tok = pl.BlockSpec((n_new, df), lambda b, c, pt: (b, 0))
    new_spec = pl.BlockSpec((None, N_DF_HEADS * n_new, width), lambda b, c, pt: (b, 0, 0))
    any_spec = pl.BlockSpec(memory_space=pl.ANY)
    return pl.pallas_call(
        functools.partial(_decode_diff_kernel, layer=layer, pages_per_step=pages_per_step, lam_init=_lam_init(layer)),
        grid_spec=pltpu.PrefetchScalarGridSpec(
            num_scalar_prefetch=1,
            grid=(n_seq, n_pages // pages_per_step),
            in_specs=[
                tok, new_spec, new_spec, any_spec, any_spec,
                pl.BlockSpec((None, 4, HEAD_DIM), lambda b, c, pt: (layer, 0, 0)),
                pl.BlockSpec((None, 1, width), lambda b, c, pt: (layer, 0, 0)),
            ],
            out_specs=tok,
            scratch_shapes=[
                pltpu.VMEM((2, pages_per_step, page_rows, width), F32),
                pltpu.VMEM((2, pages_per_step, page_rows, width), F32),
                pltpu.SemaphoreType.DMA((2, 2)),
                pltpu.VMEM((rows, 1), F32), pltpu.VMEM((rows, 1), F32), pltpu.VMEM((rows, width), F32),
                pltpu.VMEM((rows, width), BF16),
            ],
        ),
        out_shape=jax.ShapeDtypeStruct((t, df), F32),
        compiler_params=_cparams(2),
        name="decode_diff",
    )(page_table, q, k_new, v_new, cache_k, cache_v, diff_lambda, subln_w.reshape(subln_w.shape[0], 1, width))


def _merge_kernel(x_ref, osb_ref, odf_ref, g_ref, gate_ref, wos_ref, wod_ref, wout_ref, o_ref, *, transposed):
    d = x_ref.shape[1]
    g = g_ref[...]
    mix = _dot_tn if transposed else _dot
    merged = (_sigmoid(g[:, :d]) * mix(osb_ref[...].astype(BF16), wos_ref[...])
              + _sigmoid(g[:, d:]) * mix(odf_ref[...].astype(BF16), wod_ref[...]))
    o_ref[...] = x_ref[...] + gate_ref[...] * _dot(merged.astype(BF16), wout_ref[...])


def _merge(x, o_sb, o_df, g, gate, wos, wod, wout, *, tm):
    t, d = x.shape
    groups, r, _ = gate.shape
    tiles_per_group = t // (tm * groups)
    tok = lambda w: pl.BlockSpec((tm, w), lambda i: (i, 0))
    transposed = o_sb.ndim == 3
    if transposed:
        tiles_per_seq = o_sb.shape[2] // tm
        mixer = lambda o: pl.BlockSpec((None, o.shape[1], tm), lambda i: (i // tiles_per_seq, 0, i % tiles_per_seq))
    else:
        mixer = lambda o: tok(o.shape[1])
    return pl.pallas_call(
        functools.partial(_merge_kernel, transposed=transposed),
        grid=(t // tm,),
        in_specs=[
            tok(d), mixer(o_sb), mixer(o_df), tok(2 * d),
            pl.BlockSpec((None, r, d), lambda i: (i // tiles_per_group, 0, 0)),
            _resident(wos.shape), _resident(wod.shape), _resident(wout.shape),
        ],
        out_specs=tok(d),
        out_shape=jax.ShapeDtypeStruct((t, d), F32),
        compiler_params=_cparams(1),
        name="merge",
    )(x, o_sb, o_df, g, gate, wos, wod, wout)


FFN_CHUNK = 256
PROMPT_FFN_TILE = 1024
PROMPT_PROJ_TILE = 512
PROMPT_MERGE_TILE = 512
PROMPT_SB_BLOCK = 256
PROMPT_DIFF_BLOCK = 512
DECODE_PAGES_PER_STEP = 16


def kernel(x_prompt, x_sample, c_prompt, c_sample, cache_sba_k, cache_sba_v, cache_diff_k, cache_diff_v, page_table, norm_w, w_ada, b_ada, ffn_w_in, ffn_w_out, w_in, w_o_sba, w_o_diff, w_out, diff_lambda, diff_subln_w, final_norm_w):
    batch, seq, d = x_prompt.shape
    n_seq, n_new, _ = x_sample.shape
    depth = norm_w.shape[0]
    n_pool = cache_sba_k.shape[1]
    past = page_table.shape[1] * PAGE_SIZE
    sb = N_SB_HEADS * HEAD_DIM
    df = N_DF_HEADS * 2 * HEAD_DIM
    t_s = n_seq * n_new

    mod = _ada_mod(jnp.concatenate([c_prompt, c_sample], axis=0), w_ada, b_ada)
    mod = mod.reshape(depth, batch + n_seq, N_SUBLAYERS, 3, d)

    sb_kt = jnp.transpose(cache_sba_k, (0, 1, 3, 4, 2)).reshape(depth, n_pool, sb, PAGE_SIZE)
    sb_vt = jnp.transpose(cache_sba_v, (0, 1, 3, 4, 2)).reshape(depth, n_pool, sb, PAGE_SIZE)
    df_k = cache_diff_k.reshape(depth, n_pool, PAGE_SIZE * N_DF_HEADS, 2 * HEAD_DIM)
    df_v = cache_diff_v.reshape(depth, n_pool, PAGE_SIZE * N_DF_HEADS, 2 * HEAD_DIM)

    cos_p, sin_p = _rope_tables(jnp.arange(seq, dtype=jnp.int32), N_SB_HEADS)
    cos_s, sin_s = _rope_tables(jnp.tile(past + jnp.arange(n_new, dtype=jnp.int32), n_seq), N_SB_HEADS)

    xp = x_prompt.reshape(batch * seq, d)
    xs = x_sample.reshape(t_s, d)
    kv_prompt = None
    kv_sample = []
    for layer in range(depth):
        last = layer == depth - 1
        wi_a, wo_a = ffn_w_in[layer, 0].astype(BF16), ffn_w_out[layer, 0].astype(BF16)
        wi_b, wo_b = ffn_w_in[layer, 1].astype(BF16), ffn_w_out[layer, 1].astype(BF16)
        w_b = w_in[layer].astype(BF16)
        w_t = w_b[:, :3 * sb + df].T
        w_n = jnp.concatenate([w_b[:, sb:2 * sb], w_b[:, 3 * sb + df:]], axis=1)
        wos, wod, wout = w_o_sba[layer].astype(BF16), w_o_diff[layer].astype(BF16), w_out[layer].astype(BF16)
        mp = mod[layer, :batch]
        ms = jnp.repeat(mod[layer, batch:], n_new, axis=0)

        pm = lambda sub, kind: mp[:, sub, kind][:, None, :]
        xp = _ffn(xp, norm_w[layer, 0], pm(0, 0), pm(0, 1), pm(0, 2), wi_a, wo_a, final_norm_w, tm=min(PROMPT_FFN_TILE, seq), final_norm=False)
        qt_sb, qt_df, k_tok, kt, vt, kdf, vdf, g = _in_proj_prompt(
            xp, norm_w[layer, 1], pm(1, 0), pm(1, 1), cos_p, sin_p, cos_p.T, sin_p.T, w_t, w_n, kv_prompt, layer,
            depth=depth, batch=batch, seq=seq, tm=min(PROMPT_PROJ_TILE, seq))
        kv_prompt = (kt, vt, kdf, vdf)
        o_sb = _prompt_sb_t(qt_sb, k_tok, vt, layer, batch=batch, seq=seq, blk=min(PROMPT_SB_BLOCK, seq))
        o_df = _prompt_diff_t(qt_df, kdf, vdf, diff_lambda, diff_subln_w, layer, batch=batch, seq=seq, blk=min(PROMPT_DIFF_BLOCK, seq))
        xp = _merge(xp, o_sb, o_df, g, pm(1, 2), wos, wod, wout, tm=min(PROMPT_MERGE_TILE, seq))
        xp = _ffn(xp, norm_w[layer, 2], pm(2, 0), pm(2, 1), pm(2, 2), wi_b, wo_b, final_norm_w, tm=min(PROMPT_FFN_TILE, seq), final_norm=last)

        sm = lambda sub, kind: ms[:, sub, kind][None]
        xs = _ffn(xs, norm_w[layer, 0], sm(0, 0), sm(0, 1), sm(0, 2), wi_a, wo_a, final_norm_w, tm=t_s, final_norm=False)
        q_sb, k_sb, v_sb, q_df, k_df, v_df, g = _in_proj_sample(xs, norm_w[layer, 1], ms[:, 1, 0], ms[:, 1, 1], cos_s, sin_s, w_b)
        kv_sample.append((k_sb, v_sb, k_df, v_df))
        o_sb = _decode_sb(page_table, q_sb, k_sb, v_sb, sb_kt, sb_vt, layer, n_new=n_new)
        o_df = _decode_diff(
            page_table, q_df, k_df.reshape(n_seq, N_DF_HEADS * n_new, 2 * HEAD_DIM), v_df.reshape(n_seq, N_DF_HEADS * n_new, 2 * HEAD_DIM),
            df_k, df_v, diff_lambda, diff_subln_w, layer, n_new=n_new, pages_per_step=min(DECODE_PAGES_PER_STEP, page_table.shape[1]))
        xs = _merge(xs, o_sb, o_df, g, sm(1, 2), wos, wod, wout, tm=t_s)
        xs = _ffn(xs, norm_w[layer, 2], sm(2, 0), sm(2, 1), sm(2, 2), wi_b, wo_b, final_norm_w, tm=t_s, final_norm=last)

    kt, vt, kdf, vdf = kv_prompt
    stack = lambda n, shape: jnp.stack([kv[n] for kv in kv_sample]).reshape(shape)
    return (
        xp.reshape(batch, seq, d),
        xs.reshape(n_seq, n_new, d),
        jnp.transpose(kt.reshape(depth, batch, N_SB_HEADS, HEAD_DIM, seq), (0, 1, 4, 2, 3)),
        jnp.transpose(vt.reshape(depth, batch, N_SB_HEADS, HEAD_DIM, seq), (0, 1, 4, 2, 3)),
        kdf.reshape(depth, batch, seq, N_DF_HEADS, 2 * HEAD_DIM),
        vdf.reshape(depth, batch, seq, N_DF_HEADS, 2 * HEAD_DIM),
        stack(0, (depth, n_seq, n_new, N_SB_HEADS, HEAD_DIM)),
        stack(1, (depth, n_seq, n_new, N_SB_HEADS, HEAD_DIM)),
        stack(2, (depth, n_seq, n_new, N_DF_HEADS, 2 * HEAD_DIM)),
        stack(3, (depth, n_seq, n_new, N_DF_HEADS, 2 * HEAD_DIM)),
    )
```
